```python
import math
import jax
import jax.numpy as jnp
from jax import lax
import numpy as np

D_MODEL = 4096
BATCH = 4
SEQ = 4096
DEPTH = 2

MIX_WIDTH = D_MODEL
N_GROUPS = 4
GROUP_WIDTH = MIX_WIDTH // N_GROUPS
HEAD_DIM = 128
N_HEADS = GROUP_WIDTH // HEAD_DIM
DIFF_QK_DIM = HEAD_DIM // 2
CONV_WIDTH = 31
QBLK = 128
MOBA_BLOCK = 256
MOBA_TOPK = 3
MOBA_QCHUNK = 32
N_EXPERTS = 32
TOP_K = 4
D_EXPERT = D_MODEL // 8
SWIGLU_ALPHA = 1.702
SWIGLU_LIMIT = 7.0
EXPERT_ROWS = 256
RMS_EPS = 1e-6
LN_EPS = 1e-5
N_MOD = 6
IN_COLS = 11 * GROUP_WIDTH

kernel_name = 'hybrid_conv_stickbreak_diff_moba_moe'


def rms_norm(x, g):
    xf = x.astype(jnp.float32)
    y = xf * lax.rsqrt(jnp.mean(xf * xf, axis=-1, keepdims=True) + RMS_EPS)
    return (y * g.astype(jnp.float32)).astype(x.dtype)


def layer_norm(x, g, b):
    xf = x.astype(jnp.float32)
    mu = jnp.mean(xf, axis=-1, keepdims=True)
    var = jnp.mean(jnp.square(xf - mu), axis=-1, keepdims=True)
    y = (xf - mu) * lax.rsqrt(var + LN_EPS) * g.astype(jnp.float32) + b.astype(jnp.float32)
    return y.astype(x.dtype)


def to_heads(t):
    B, S, _ = t.shape
    return t.reshape(B, S, N_HEADS, HEAD_DIM).transpose(0, 2, 1, 3)


def from_heads(t):
    B, H, S, d = t.shape
    return t.transpose(0, 2, 1, 3).reshape(B, S, H * d)


def query_blocks(q, blk):
    B, H, S = q.shape[:3]
    qb = q.reshape((B, H, S // blk, blk) + q.shape[3:])
    return jnp.moveaxis(qb, 2, 0)


def merge_blocks(o):
    nb, B, H, blk, d = o.shape
    return jnp.moveaxis(o, 0, 2).reshape(B, H, nb * blk, d)


def conformer_conv(u2, conv_w, conv_b, ln_g, ln_b):
    val, gate = u2[..., :GROUP_WIDTH], u2[..., GROUP_WIDTH:]
    u = val * jax.nn.sigmoid(gate)
    y = lax.conv_general_dilated(u, conv_w[:, None, :], window_strides=(1,),
                                 padding=[(CONV_WIDTH - 1, 0)],
                                 dimension_numbers=('NWC', 'WIO', 'NWC'),
                                 feature_group_count=GROUP_WIDTH) + conv_b
    return jax.nn.silu(layer_norm(y, ln_g, ln_b))


def stick_breaking(q, k, v):
    S = q.shape[2]
    scale = HEAD_DIM ** -0.5
    kpos = jnp.arange(S)

    def block(args):
        qi, i = args
        tpos = i * QBLK + jnp.arange(QBLK)
        z = jnp.einsum('bhqd,bhkd->bhqk', qi, k).astype(jnp.float32) * scale
        past = kpos[None, :] < tpos[:, None]
        u = jnp.where(past, jax.nn.softplus(z), 0.0)
        tail = lax.cumsum(u, axis=3, reverse=True)
        a = jnp.exp(jnp.where(past, z - tail, -jnp.inf))
        return jnp.einsum('bhqk,bhkd->bhqd', a.astype(v.dtype), v)

    return merge_blocks(lax.map(block, (query_blocks(q, QBLK), jnp.arange(S // QBLK))))


def diff_attention(q, k, v, lam, norm_g, lam_init):
    S = q.shape[2]
    scale = DIFF_QK_DIM ** -0.5
    kpos = jnp.arange(S)

    def block(args):
        qi, i = args
        tpos = i * QBLK + jnp.arange(QBLK)
        s = jnp.einsum('bhqmd,bhkmd->bhmqk', qi, k).astype(jnp.float32) * scale
        s = jnp.where(kpos[None, :] <= tpos[:, None], s, -jnp.inf)
        p = jax.nn.softmax(s, axis=-1)
        w = p[:, :, 0] - lam * p[:, :, 1]
        return jnp.einsum('bhqk,bhkd->bhqd', w.astype(v.dtype), v)

    o = merge_blocks(lax.map(block, (query_blocks(q, QBLK), jnp.arange(S // QBLK))))
    return rms_norm(o, norm_g) * (1.0 - lam_init)


def moba(q, k, v):
    B, H, S, d = q.shape
    scale = HEAD_DIM ** -0.5
    nkb = -(-S // MOBA_BLOCK)
    pad = ((0, 0), (0, 0), (0, nkb * MOBA_BLOCK - S), (0, 0))
    kp = jnp.pad(k, pad)
    vp = jnp.pad(v, pad)
    kblk = kp.reshape(B, H, nkb, MOBA_BLOCK, d)
    vblk = vp.reshape(B, H, nkb, MOBA_BLOCK, d)
    kmean = jnp.mean(kblk.astype(jnp.float32), axis=3)
    topk = min(MOBA_TOPK, nkb)
    bi = jnp.arange(B)[:, None, None, None]
    hi = jnp.arange(H)[None, :, None, None]
    blk_ids = jnp.arange(nkb)
    own_off = jnp.arange(MOBA_BLOCK)

    def chunk(args):
        qi, i = args
        tpos = i * MOBA_QCHUNK + jnp.arange(MOBA_QCHUNK)
        own = (i * MOBA_QCHUNK) // MOBA_BLOCK
        gate = jnp.einsum('bhqd,bhnd->bhqn', qi.astype(jnp.float32), kmean)
        gate = jnp.where(blk_ids < own, gate, -jnp.inf)
        gval, gidx = lax.top_k(gate, topk)
        sel_ok = jnp.isfinite(gval)
        kg = kblk[bi, hi, gidx]
        vg = vblk[bi, hi, gidx]
        s_past = jnp.einsum('bhqd,bhqnkd->bhqnk', qi, kg).astype(jnp.float32) * scale
        s_past = jnp.where(sel_ok[..., None], s_past, -jnp.inf).reshape(B, H, MOBA_QCHUNK, topk * MOBA_BLOCK)
        kown = lax.dynamic_slice_in_dim(kp, own * MOBA_BLOCK, MOBA_BLOCK, axis=2)
        vown = lax.dynamic_slice_in_dim(vp, own * MOBA_BLOCK, MOBA_BLOCK, axis=2)
        s_own = jnp.einsum('bhqd,bhkd->bhqk', qi, kown).astype(jnp.float32) * scale
        s_own = jnp.where((own * MOBA_BLOCK + own_off)[None, :] <= tpos[:, None], s_own, -jnp.inf)
        p = jax.nn.softmax(jnp.concatenate([s_past, s_own], axis=-1), axis=-1)
        p_past = p[..., :topk * MOBA_BLOCK].reshape(B, H, MOBA_QCHUNK, topk, MOBA_BLOCK)
        p_own = p[..., topk * MOBA_BLOCK:]
        return (jnp.einsum('bhqnk,bhqnkd->bhqd', p_past.astype(v.dtype), vg)
                + jnp.einsum('bhqk,bhkd->bhqd', p_own.astype(v.dtype), vown))

    return merge_blocks(lax.map(chunk, (query_blocks(q, MOBA_QCHUNK), jnp.arange(S // MOBA_QCHUNK))))


def hybrid_mixer(h, w_in, w_out, conv_w, conv_b, conv_ln_g, conv_ln_b, diff_lambda, diff_norm_g, lam_init):
    B, S, _ = h.shape
    G = GROUP_WIDTH
    p = h @ w_in
    y_conv = conformer_conv(p[..., 0:2 * G], conv_w, conv_b, conv_ln_g, conv_ln_b)
    y_sb = from_heads(stick_breaking(to_heads(p[..., 2 * G:3 * G]), to_heads(p[..., 3 * G:4 * G]),
                                     to_heads(p[..., 4 * G:5 * G])))
    dq = p[..., 5 * G:6 * G].reshape(B, S, N_HEADS, 2, DIFF_QK_DIM).transpose(0, 2, 1, 3, 4)
    dk = p[..., 6 * G:7 * G].reshape(B, S, N_HEADS, 2, DIFF_QK_DIM).transpose(0, 2, 1, 3, 4)
    dv = to_heads(p[..., 7 * G:8 * G])
    dl = diff_lambda.astype(jnp.float32)
    lam = jnp.exp(jnp.sum(dl[0] * dl[1])) - jnp.exp(jnp.sum(dl[2] * dl[3])) + lam_init
    y_diff = from_heads(diff_attention(dq, dk, dv, lam, diff_norm_g, lam_init))
    y_moba = from_heads(moba(to_heads(p[..., 8 * G:9 * G]), to_heads(p[..., 9 * G:10 * G]),
                             to_heads(p[..., 10 * G:11 * G])))
    return jnp.concatenate([y_conv, y_sb, y_diff, y_moba], axis=-1) @ w_out


def moe_ffn(h, w_router, b_router, w1, b1, w2, b2):
    B, S, D = h.shape
    N = B * S
    NK = N * TOP_K
    n_blocks = (NK + N_EXPERTS * (EXPERT_ROWS - 1) + EXPERT_ROWS - 1) // EXPERT_ROWS
    rows = n_blocks * EXPERT_ROWS
    ht = h.reshape(N, D)
    logits = (ht @ w_router + b_router).astype(jnp.float32)
    top_logits, top_idx = lax.top_k(logits, TOP_K)
    gates = jax.nn.softmax(top_logits, axis=-1)
    e_flat = top_idx.reshape(NK)
    tok_flat = jnp.arange(NK, dtype=jnp.int32) // TOP_K
    order = jnp.argsort(e_flat)
    e_sorted = e_flat[order]
    counts = jnp.zeros((N_EXPERTS,), jnp.int32).at[e_flat].add(1)
    starts = jnp.cumsum(counts) - counts
    padded = (counts + EXPERT_ROWS - 1) // EXPERT_ROWS * EXPERT_ROWS
    pends = jnp.cumsum(padded)
    pstarts = pends - padded
    dest = pstarts[e_sorted] + jnp.arange(NK, dtype=jnp.int32) - starts[e_sorted]
    buf_tok = jnp.full((rows,), N, jnp.int32).at[dest].set(tok_flat[order])
    buf_gate = jnp.zeros((rows,), jnp.float32).at[dest].set(gates.reshape(NK)[order])
    blk_start = jnp.arange(n_blocks, dtype=jnp.int32) * EXPERT_ROWS
    blk_expert = jnp.minimum(jnp.sum(pends[None, :] <= blk_start[:, None], axis=1),
                             N_EXPERTS - 1).astype(jnp.int32)
    h_pad = jnp.concatenate([ht, jnp.zeros((1, D), ht.dtype)], axis=0)

    def expert_rows(args):
        tok, e = args
        a = h_pad[tok] @ w1[e] + b1[e]
        g = jnp.minimum(a[:, :D_EXPERT], SWIGLU_LIMIT)
        lin = jnp.clip(a[:, D_EXPERT:], -SWIGLU_LIMIT, SWIGLU_LIMIT)
        return (g * jax.nn.sigmoid(SWIGLU_ALPHA * g) * (lin + 1.0)) @ w2[e] + b2[e]

    y_rows = lax.map(expert_rows, (buf_tok.reshape(n_blocks, EXPERT_ROWS), blk_expert))
    y_rows = y_rows.reshape(rows, D) * buf_gate[:, None].astype(y_rows.dtype)
    y = jax.ops.segment_sum(y_rows, buf_tok, num_segments=N + 1)[:N]
    return y.reshape(B, S, D)


def setup_inputs(seed: int = 0) -> dict:
    key = jax.random.key(seed)
    ks = jax.random.split(key, 24)
    f32 = jnp.float32
    D, G, E, F = D_MODEL, GROUP_WIDTH, N_EXPERTS, D_EXPERT

    def nrm(k, shape, s):
        return jax.random.normal(k, shape, f32) * s

    return {
        'x': nrm(ks[0], (BATCH, SEQ, D), 1.0),
        'c': nrm(ks[1], (BATCH, D), 1.0),
        'w_ada': nrm(ks[2], (D, N_MOD * D), 0.5 * D ** -0.5),
        'b_ada': nrm(ks[3], (N_MOD * D,), 0.02),
        'ada_table': nrm(ks[4], (DEPTH, N_MOD, D), 0.1),
        'norm_pre': 1.0 + nrm(ks[5], (DEPTH, 2, D), 0.02),
        'norm_post': 1.0 + nrm(ks[6], (DEPTH, 2, D), 0.02),
        'w_in': nrm(ks[7], (DEPTH, D, IN_COLS), D ** -0.5),
        'w_out': nrm(ks[8], (DEPTH, MIX_WIDTH, D), MIX_WIDTH ** -0.5),
        'conv_w': nrm(ks[9], (DEPTH, CONV_WIDTH, G), CONV_WIDTH ** -0.5),
        'conv_b': nrm(ks[10], (DEPTH, G), 0.02),
        'conv_ln_g': 1.0 + nrm(ks[11], (DEPTH, G), 0.02),
        'conv_ln_b': nrm(ks[12], (DEPTH, G), 0.02),
        'diff_lambda': nrm(ks[13], (DEPTH, 4, DIFF_QK_DIM), 0.1),
        'diff_norm_g': 1.0 + nrm(ks[14], (DEPTH, HEAD_DIM), 0.02),
        'w_router': nrm(ks[15], (DEPTH, D, E), D ** -0.5),
        'b_router': nrm(ks[16], (DEPTH, E), 0.01),
        'w1': nrm(ks[17], (DEPTH, E, D, 2 * F), D ** -0.5),
        'b1': nrm(ks[18], (DEPTH, E, 2 * F), 0.02),
        'w2': nrm(ks[19], (DEPTH, E, F, D), F ** -0.5),
        'b2': nrm(ks[20], (DEPTH, E, D), 0.02),
    }


def reference(x, c, w_ada, b_ada, ada_table, norm_pre, norm_post, w_in, w_out, conv_w, conv_b,
              conv_ln_g, conv_ln_b, diff_lambda, diff_norm_g, w_router, b_router, w1, b1, w2, b2):
    B = x.shape[0]
    cond = (jax.nn.silu(c) @ w_ada + b_ada).reshape(B, N_MOD, D_MODEL)
    for l in range(DEPTH):
        mod = cond + ada_table[l][None]
        shift_a, scale_a, gate_a, shift_f, scale_f, gate_f = (mod[:, i, None, :] for i in range(N_MOD))
        lam_init = 0.8 - 0.6 * math.exp(-0.3 * l)
        h = rms_norm(x, norm_pre[l, 0]) * (1.0 + scale_a) + shift_a
        y = hybrid_mixer(h, w_in[l], w_out[l], conv_w[l], conv_b[l], conv_ln_g[l], conv_ln_b[l],
                         diff_lambda[l], diff_norm_g[l], lam_init)
        x = x + gate_a * rms_norm(y, norm_post[l, 0])
        h = rms_norm(x, norm_pre[l, 1]) * (1.0 + scale_f) + shift_f
        y = moe_ffn(h, w_router[l], b_router[l], w1[l], b1[l], w2[l], b2[l])
        x = x + gate_f * rms_norm(y, norm_post[l, 1])
    return x
```

```python
import functools
import math

import jax
import jax.numpy as jnp
from jax import lax
from jax.experimental import pallas as pl
from jax.experimental.pallas import tpu as pltpu

F32 = jnp.float32
BF16 = jnp.bfloat16
U32 = jnp.uint32
I32 = jnp.int32

N_GROUPS = 4
HEAD_DIM = 128
DIFF_QK_DIM = HEAD_DIM // 2
CONV_WIDTH = 31
MOBA_BLOCK = 256
MOBA_TOPK = 3
N_EXPERTS = 32
TOP_K = 4
SWIGLU_ALPHA = 1.702
SWIGLU_LIMIT = 7.0
RMS_EPS = 1e-6
LN_EPS = 1e-5
N_MOD = 6
LANES = 128
CONV_HALO = 32
NEG_BIG = -1e30
MIB = 1024 * 1024


def _cparams(sem, vmem_mib):
    return pltpu.CompilerParams(dimension_semantics=sem, vmem_limit_bytes=vmem_mib * MIB)


def _rms(xf, g):
    return xf * lax.rsqrt(jnp.mean(xf * xf, axis=-1, keepdims=True) + RMS_EPS) * g


def _pack_halves(a):
    k = a.shape[1] // 2
    lo = lax.bitcast_convert_type(a[:, :k].astype(BF16).astype(F32), U32)
    hi = lax.bitcast_convert_type(a[:, k:].astype(BF16).astype(F32), U32)
    return (lo >> 16) | (hi & jnp.uint32(0xFFFF0000))


def _unpack_halves(w):
    lo = lax.bitcast_convert_type(w << 16, F32).astype(BF16)
    hi = lax.bitcast_convert_type(w & jnp.uint32(0xFFFF0000), F32).astype(BF16)
    return lo, hi


def _ada_kernel(c_ref, w_ref, b_ref, o_ref):
    c = c_ref[...]
    a = (c * jax.nn.sigmoid(c)).astype(BF16)
    o_ref[...] = jnp.dot(a, w_ref[...].astype(BF16), preferred_element_type=F32) + b_ref[...]


def _ada(c, w_ada, b_ada):
    B, D = c.shape
    n_out = w_ada.shape[1]
    tn = 512
    rows = 8
    c8 = jnp.zeros((rows, D), F32).at[:B].set(c)
    out = pl.pallas_call(
        _ada_kernel,
        out_shape=jax.ShapeDtypeStruct((rows, n_out), F32),
        grid=(n_out // tn,),
        in_specs=[pl.BlockSpec((rows, D), lambda j: (0, 0)),
                  pl.BlockSpec((D, tn), lambda j: (0, j)),
                  pl.BlockSpec((1, tn), lambda j: (0, j))],
        out_specs=pl.BlockSpec((rows, tn), lambda j: (0, j)),
        compiler_params=_cparams(("arbitrary",), 40),
        name="ada",
    )(c8, w_ada, b_ada.reshape(1, n_out))
    return out[:B].reshape(B, N_MOD, D)


def _router_topk(hf, wr_hi_ref, wr_lo_ref, br_ref):
    h_hi = hf.astype(BF16)
    h_lo = (hf - h_hi.astype(F32)).astype(BF16)
    logits = (jnp.dot(h_hi, wr_hi_ref[...], preferred_element_type=F32)
              + jnp.dot(h_lo, wr_hi_ref[...], preferred_element_type=F32)
              + jnp.dot(h_hi, wr_lo_ref[...], preferred_element_type=F32)) + br_ref[...]
    lane = lax.broadcasted_iota(I32, logits.shape, 1)
    lane_f = lane.astype(F32)
    l = jnp.where(lane < N_EXPERTS, logits, -jnp.inf)
    idx_out = jnp.zeros(logits.shape, F32)
    val_out = jnp.full(logits.shape, -jnp.inf, F32)
    m0 = None
    for k in range(TOP_K):
        m = jnp.max(l, axis=1, keepdims=True)
        am = jnp.min(jnp.where(l == m, lane_f, float(LANES)), axis=1, keepdims=True)
        idx_out = jnp.where(lane == k, am, idx_out)
        val_out = jnp.where(lane == k, m, val_out)
        l = jnp.where(lane_f == am, -jnp.inf, l)
        if k == 0:
            m0 = m
    e = jnp.where(lane < TOP_K, jnp.exp(val_out - m0), 0.0)
    gates = e / jnp.sum(e, axis=1, keepdims=True)
    return idx_out.astype(I32), gates


def _post_kernel(*refs, mode, nxt, i_gate, i_shift, i_scale):
    it = iter(refs)
    x_ref = next(it)
    if mode == "dense":
        y_ref = next(it)
    elif mode == "moe":
        y4_ref = next(it)
        g4_ref = next(it)
    cond_ref = next(it)
    tab_ref = next(it)
    tabn_ref = next(it)
    gpost_ref = next(it) if mode != "none" else None
    gpre_ref = next(it) if nxt != "none" else None
    if nxt == "router":
        wr_hi_ref, wr_lo_ref, br_ref = next(it), next(it), next(it)
    outs = list(it)

    mod = cond_ref[0] + tab_ref[...]
    modn = cond_ref[0] + tabn_ref[...]
    x = x_ref[...]
    if mode == "dense":
        y = y_ref[...].astype(F32)
    elif mode == "moe":
        g4 = g4_ref[...]
        y = None
        for k in range(TOP_K):
            lo, hi = _unpack_halves(y4_ref[k])
            yk = jnp.concatenate([lo.astype(F32), hi.astype(F32)], axis=1) * g4[:, k:k + 1]
            y = yk if y is None else y + yk
    o = 0
    if mode != "none":
        x = x + mod[i_gate:i_gate + 1, :] * _rms(y, gpost_ref[...])
        outs[o][...] = x
        o += 1
    if nxt != "none":
        hf = _rms(x, gpre_ref[...]) * (1.0 + modn[i_scale:i_scale + 1, :]) + modn[i_shift:i_shift + 1, :]
        if nxt == "plain":
            outs[o][...] = hf.astype(BF16)
        else:
            outs[o][...] = _pack_halves(hf)
            idx, gates = _router_topk(hf, wr_hi_ref, wr_lo_ref, br_ref)
            outs[o + 1][...] = idx
            outs[o + 2][...] = gates


def _post(x, cond, tab, tab_next, *, S, mode, nxt, y=None, y4=None, g4=None, g_post=None, g_pre=None,
          router=None, i_gate=0, i_shift=0, i_scale=0, tm=256):
    N, D = x.shape
    nb = S // tm
    row = lambda i: (i, 0)
    args, specs = [x], [pl.BlockSpec((tm, D), row)]
    if mode == "dense":
        args.append(y)
        specs.append(pl.BlockSpec((tm, D), row))
    elif mode == "moe":
        args += [y4, g4]
        specs += [pl.BlockSpec((TOP_K, tm, D // 2), lambda i: (0, i, 0)), pl.BlockSpec((tm, LANES), row)]
    args += [cond, tab, tab_next]
    specs += [pl.BlockSpec((1, N_MOD, D), lambda i: (i // nb, 0, 0)), pl.BlockSpec((N_MOD, D), lambda i: (0, 0)),
              pl.BlockSpec((N_MOD, D), lambda i: (0, 0))]
    if mode != "none":
        args.append(g_post.reshape(1, D))
        specs.append(pl.BlockSpec((1, D), lambda i: (0, 0)))
    if nxt != "none":
        args.append(g_pre.reshape(1, D))
        specs.append(pl.BlockSpec((1, D), lambda i: (0, 0)))
    if nxt == "router":
        args += list(router)
        specs += [pl.BlockSpec((D, LANES), lambda i: (0, 0)), pl.BlockSpec((D, LANES), lambda i: (0, 0)),
                  pl.BlockSpec((1, LANES), lambda i: (0, 0))]
    out_shape, out_specs = [], []
    if mode != "none":
        out_shape.append(jax.ShapeDtypeStruct((N, D), F32))
        out_specs.append(pl.BlockSpec((tm, D), row))
    if nxt == "plain":
        out_shape.append(jax.ShapeDtypeStruct((N, D), BF16))
        out_specs.append(pl.BlockSpec((tm, D), row))
    elif nxt == "router":
        out_shape += [jax.ShapeDtypeStruct((N, D // 2), U32), jax.ShapeDtypeStruct((N, LANES), I32),
                      jax.ShapeDtypeStruct((N, LANES), F32)]
        out_specs += [pl.BlockSpec((tm, D // 2), row), pl.BlockSpec((tm, LANES), row), pl.BlockSpec((tm, LANES), row)]
    return pl.pallas_call(
        functools.partial(_post_kernel, mode=mode, nxt=nxt, i_gate=i_gate, i_shift=i_shift, i_scale=i_scale),
        out_shape=out_shape, grid=(N // tm,), in_specs=specs, out_specs=out_specs,
        compiler_params=_cparams(("arbitrary",), 56),
        name=f"post_{mode}_{nxt}",
    )(*args)


def _mm_kernel(*refs, n_a):
    a_refs, w_ref, o_ref = refs[:n_a], refs[n_a], refs[n_a + 1]
    acc, off = None, 0
    for a in a_refs:
        kg = a.shape[1]
        part = jnp.dot(a[...], w_ref[off:off + kg, :], preferred_element_type=F32)
        acc = part if acc is None else acc + part
        off += kg
    o_ref[...] = acc.astype(o_ref.dtype)


def _matmul(a_list, w, out_dtype, tm, tn):
    N = a_list[0].shape[0]
    K, n_out = w.shape
    specs = [pl.BlockSpec((tm, a.shape[1]), lambda i, j: (i, 0)) for a in a_list]
    specs.append(pl.BlockSpec((K, tn), lambda i, j: (0, j)))
    return pl.pallas_call(
        functools.partial(_mm_kernel, n_a=len(a_list)),
        out_shape=jax.ShapeDtypeStruct((N, n_out), out_dtype),
        grid=(N // tm, n_out // tn), in_specs=specs,
        out_specs=pl.BlockSpec((tm, tn), lambda i, j: (i, j)),
        compiler_params=_cparams(("arbitrary", "arbitrary"), 56),
        name="matmul",
    )(*a_list, w)


def _conv_kernel(val_ref, gate_ref, pval_ref, pgate_ref, w_ref, b_ref, lg_ref, lb_ref, o_ref, ubuf, ybuf, *, ts):
    G = o_ref.shape[1]
    i = pl.program_id(1)
    v = val_ref[...].astype(F32)
    g = gate_ref[...].astype(F32)
    ubuf[CONV_HALO:CONV_HALO + ts, :] = v * jax.nn.sigmoid(g)
    pv = pval_ref[...].astype(F32)
    pg = pgate_ref[...].astype(F32)
    ubuf[0:CONV_HALO, :] = jnp.where(i > 0, pv * jax.nn.sigmoid(pg), 0.0)
    base = CONV_HALO - (CONV_WIDTH - 1)
    rs = 128

    def chan(c, carry):
        c0 = pl.multiple_of(c * LANES, LANES)
        for r0 in range(0, ts, rs):
            acc = jnp.zeros((rs, LANES), F32)
            for j in range(CONV_WIDTH):
                acc = acc + w_ref[j:j + 1, pl.ds(c0, LANES)] * ubuf[base + j + r0:base + j + r0 + rs, pl.ds(c0, LANES)]
            ybuf[r0:r0 + rs, pl.ds(c0, LANES)] = acc
        return carry

    lax.fori_loop(0, G // LANES, chan, 0)
    for r0 in range(0, ts, 64):
        y = ybuf[r0:r0 + 64, :] + b_ref[...]
        mu = jnp.mean(y, axis=-1, keepdims=True)
        d = y - mu
        var = jnp.mean(d * d, axis=-1, keepdims=True)
        z = d * lax.rsqrt(var + LN_EPS) * lg_ref[...] + lb_ref[...]
        o_ref[r0:r0 + 64, :] = (z * jax.nn.sigmoid(z)).astype(o_ref.dtype)


def _conv_mixer(p, conv_w, conv_b, ln_g, ln_b, *, B, S, G, ts=256):
    nb = S // ts
    hb = ts // CONV_HALO
    wpad = jnp.zeros((CONV_HALO, G), F32).at[:CONV_WIDTH].set(conv_w)
    cur = lambda col: pl.BlockSpec((ts, G), lambda b, i: (b * nb + i, col))
    prev = lambda col: pl.BlockSpec((CONV_HALO, G), lambda b, i: (jnp.maximum((b * nb + i) * hb - 1, 0), col))
    vec = pl.BlockSpec((1, G), lambda b, i: (0, 0))
    return pl.pallas_call(
        functools.partial(_conv_kernel, ts=ts),
        out_shape=jax.ShapeDtypeStruct((B * S, G), BF16),
        grid=(B, nb),
        in_specs=[cur(0), cur(1), prev(0), prev(1), pl.BlockSpec((CONV_HALO, G), lambda b, i: (0, 0)), vec, vec, vec],
        out_specs=pl.BlockSpec((ts, G), lambda b, i: (b * nb + i, 0)),
        scratch_shapes=[pltpu.VMEM((ts + CONV_HALO, G), F32), pltpu.VMEM((ts, G), F32)],
        compiler_params=_cparams(("arbitrary", "arbitrary"), 32),
        name="conv_mixer",
    )(p, p, p, p, wpad, conv_b.reshape(1, G), ln_g.reshape(1, G), ln_b.reshape(1, G))


_NT = (((1,), (1,)), ((), ()))


def _head_specs(S, tq, nq, c_q, c_k, c_v):
    return [pl.BlockSpec((tq, HEAD_DIM), lambda b, h, i: (b * nq + i, c_q + h)),
            pl.BlockSpec((S, HEAD_DIM), lambda b, h, i: (b, c_k + h)),
            pl.BlockSpec((S, HEAD_DIM), lambda b, h, i: (b, c_v + h))]


def _sb_kernel(q_ref, k_ref, v_ref, o_ref, *, t):
    i = pl.program_id(2)
    q = q_ref[...]
    scale = HEAD_DIM ** -0.5
    row = lax.broadcasted_iota(I32, (t, t), 0)
    col = lax.broadcasted_iota(I32, (t, t), 1)
    tri = jnp.where(row >= col, 1.0, 0.0).astype(BF16)
    past = col < row

    def block(j, carry, acc, diag):
        j0 = pl.multiple_of(j * t, t)
        kj = k_ref[pl.ds(j0, t), :]
        vj = v_ref[pl.ds(j0, t), :]
        z = lax.dot_general(q, kj, _NT, preferred_element_type=F32) * scale
        sp = jnp.maximum(z, 0.0) + jnp.log(1.0 + jnp.exp(-jnp.abs(z)))
        u = jnp.where(past, sp, 0.0) if diag else sp
        u_hi = u.astype(BF16)
        u_lo = (u - u_hi.astype(F32)).astype(BF16)
        tl = jnp.dot(u_hi, tri, preferred_element_type=F32) + jnp.dot(u_lo, tri, preferred_element_type=F32)
        e = z - (tl + carry)
        if diag:
            e = jnp.where(past, e, -jnp.inf)
        a = jnp.exp(e)
        acc = acc + jnp.dot(a.astype(BF16), vj, preferred_element_type=F32)
        return carry + tl[:, 0:1], acc

    carry, acc = block(i, jnp.zeros((t, 1), F32), jnp.zeros((t, HEAD_DIM), F32), True)

    def body(jj, ca):
        return block(i - 1 - jj, ca[0], ca[1], False)

    carry, acc = lax.fori_loop(0, i, body, (carry, acc))
    o_ref[...] = acc.astype(o_ref.dtype)


def _diff_kernel(q_ref, k_ref, v_ref, dl_ref, g_ref, o_ref, *, t, lam_init):
    i = pl.program_id(2)
    q = q_ref[...]
    scale = DIFF_QK_DIM ** -0.5
    lane = lax.broadcasted_iota(I32, q.shape, 1)
    zero = jnp.zeros_like(q)
    qs = (jnp.where(lane < DIFF_QK_DIM, q, zero), jnp.where(lane >= DIFF_QK_DIM, q, zero))
    row = lax.broadcasted_iota(I32, (t, t), 0)
    col = lax.broadcasted_iota(I32, (t, t), 1)
    causal = col <= row
    dl = dl_ref[...]
    lam = (jnp.exp(jnp.sum(dl[0:1] * dl[1:2], axis=1, keepdims=True))
           - jnp.exp(jnp.sum(dl[2:3] * dl[3:4], axis=1, keepdims=True)) + lam_init)

    def block(j, st, diag):
        j0 = pl.multiple_of(j * t, t)
        kj = k_ref[pl.ds(j0, t), :]
        vj = v_ref[pl.ds(j0, t), :]
        new = []
        for mi in range(2):
            m, l, acc = st[mi]
            s = lax.dot_general(qs[mi], kj, _NT, preferred_element_type=F32) * scale
            if diag:
                s = jnp.where(causal, s, NEG_BIG)
            m_new = jnp.maximum(m, jnp.max(s, axis=1, keepdims=True))
            alpha = jnp.exp(m - m_new)
            pr = jnp.exp(s - m_new)
            l = alpha * l + jnp.sum(pr, axis=1, keepdims=True)
            acc = alpha * acc + jnp.dot(pr.astype(BF16), vj, preferred_element_type=F32)
            new.append((m_new, l, acc))
        return tuple(new)

    init = tuple((jnp.full((t, 1), NEG_BIG, F32), jnp.zeros((t, 1), F32), jnp.zeros((t, HEAD_DIM), F32))
                 for _ in range(2))
    st = block(i, init, True)
    st = lax.fori_loop(0, i, lambda j, s: block(j, s, False), st)
    (_, l0, a0), (_, l1, a1) = st
    o = a0 / l0 - lam * (a1 / l1)
    o_ref[...] = (_rms(o, g_ref[...]) * (1.0 - lam_init)).astype(o_ref.dtype)


def _moba_kernel(q_ref, k_ref, v_ref, o_ref, km_hi, km_lo, *, t, nkb):
    own = pl.program_id(2)
    scale = HEAD_DIM ** -0.5
    topk = min(MOBA_TOPK, nkb)

    @pl.when(own == 0)
    def _():
        km_hi[...] = jnp.zeros_like(km_hi)
        km_lo[...] = jnp.zeros_like(km_lo)
        for blk in range(nkb):
            km = jnp.mean(k_ref[blk * t:(blk + 1) * t, :].astype(F32), axis=0, keepdims=True)
            hi = km.astype(BF16)
            km_hi[blk:blk + 1, :] = hi
            km_lo[blk:blk + 1, :] = (km - hi.astype(F32)).astype(BF16)

    q = q_ref[...]
    gate = (lax.dot_general(q, km_hi[...], _NT, preferred_element_type=F32)
            + lax.dot_general(q, km_lo[...], _NT, preferred_element_type=F32))
    lane = lax.broadcasted_iota(I32, gate.shape, 1)
    rank = jnp.zeros(gate.shape, F32)
    for jb in range(nkb):
        c = gate[:, jb:jb + 1]
        beats = (c > gate) | ((c == gate) & (lane > jb))
        rank = rank + jnp.where(beats, jnp.where(jb < own, 1.0, 0.0), 0.0)
    sel = jnp.where((rank < float(topk)) & (lane < own), 1.0, 0.0)

    row = lax.broadcasted_iota(I32, (t, t), 0)
    col = lax.broadcasted_iota(I32, (t, t), 1)

    def attend(j, m, l, acc, mask):
        j0 = pl.multiple_of(j * t, t)
        kj = k_ref[pl.ds(j0, t), :]
        vj = v_ref[pl.ds(j0, t), :]
        s = lax.dot_general(q, kj, _NT, preferred_element_type=F32) * scale
        s = jnp.where(mask, s, NEG_BIG)
        m_new = jnp.maximum(m, jnp.max(s, axis=1, keepdims=True))
        alpha = jnp.exp(m - m_new)
        pr = jnp.exp(s - m_new)
        l = alpha * l + jnp.sum(pr, axis=1, keepdims=True)
        acc = alpha * acc + jnp.dot(pr.astype(BF16), vj, preferred_element_type=F32)
        return m_new, l, acc

    st = attend(own, jnp.full((t, 1), NEG_BIG, F32), jnp.zeros((t, 1), F32), jnp.zeros((t, HEAD_DIM), F32),
                col <= row)

    def body(j, st):
        picked = jnp.max(jnp.where(lane == j, sel, 0.0), axis=1, keepdims=True) > 0.5
        return attend(j, st[0], st[1], st[2], picked)

    _, l, acc = lax.fori_loop(0, own, body, st)
    o_ref[...] = (acc / l).astype(o_ref.dtype)


def _attention(kernel_fn, p, extra, extra_specs, scratch, *, B, S, G, c_q, c_k, c_v, t, name):
    H = G // HEAD_DIM
    nq = S // t
    return pl.pallas_call(
        kernel_fn,
        out_shape=jax.ShapeDtypeStruct((B * S, G), BF16),
        grid=(B, H, nq),
        in_specs=_head_specs(S, t, nq, c_q, c_k, c_v) + extra_specs,
        out_specs=pl.BlockSpec((t, HEAD_DIM), lambda b, h, i: (b * nq + i, h)),
        scratch_shapes=scratch,
        compiler_params=_cparams(("arbitrary", "arbitrary", "arbitrary"), 32),
        name=name,
    )(p, p, p, *extra)


def _moe_plan(top_idx, N, tm):
    E, NK = N_EXPERTS, N * TOP_K
    nblocks = (NK + E * (tm - 1) + tm - 1) // tm
    e_flat = top_idx.reshape(NK)
    order = jnp.argsort(e_flat, stable=True).astype(I32)
    counts = jnp.sum((e_flat[:, None] == jnp.arange(E, dtype=I32)[None, :]).astype(I32), axis=0)
    starts = jnp.cumsum(counts) - counts
    nblk_e = (counts + tm - 1) // tm
    bend = jnp.cumsum(nblk_e)
    bstart = bend - nblk_e
    b = jnp.arange(nblocks, dtype=I32)
    blk_e = jnp.minimum(jnp.sum((bend[None, :] <= b[:, None]).astype(I32), axis=1), E - 1).astype(I32)
    blk_off = (b - bstart[blk_e]) * tm
    blk_nv = jnp.where(b < bend[-1], jnp.clip(counts[blk_e] - blk_off, 0, tm), 0).astype(I32)
    ri = jnp.arange(tm, dtype=I32)
    valid = ri[None, :] < blk_nv[:, None]
    src = jnp.clip(starts[blk_e][:, None] + blk_off[:, None] + ri[None, :], 0, NK - 1)
    flat = order[src]
    row_tok = jnp.where(valid, flat // TOP_K, 0).astype(I32)
    row_slot = jnp.where(valid, (flat % TOP_K) * N + flat // TOP_K, -1).astype(I32)
    return blk_e, blk_nv, row_tok, row_slot


def _rowcopy_kernel(idx_hbm, src_hbm, dst_hbm, idx_smem, isem, csem, *, ch, indexed_src):
    s = pl.program_id(0)
    ns = pl.num_programs(0)
    slot = s % 2

    def idx_copy(step, sl):
        return pltpu.make_async_copy(idx_hbm.at[step], idx_smem.at[sl], isem.at[sl])

    @pl.when(s == 0)
    def _():
        idx_copy(0, 0).start()

    idx_copy(s, slot).wait()

    @pl.when(s + 1 < ns)
    def _():
        idx_copy(s + 1, 1 - slot).start()

    def row_copy(r, t):
        seq = s * ch + r
        if indexed_src:
            return pltpu.make_async_copy(src_hbm.at[pl.ds(t, 1)], dst_hbm.at[pl.ds(seq, 1)], csem)
        return pltpu.make_async_copy(src_hbm.at[pl.ds(seq, 1)], dst_hbm.at[pl.ds(t, 1)], csem)

    def issue(r, c):
        t = idx_smem[slot, r]

        @pl.when(t >= 0)
        def _():
            row_copy(r, t).start()
        return c

    def drain(r, c):
        t = idx_smem[slot, r]

        @pl.when(t >= 0)
        def _():
            row_copy(r, t).wait()
        return c

    lax.fori_loop(0, ch, issue, 0)
    lax.fori_loop(0, ch, drain, 0)


def _rowcopy(idx, src, n_dst, *, indexed_src, ch=512):
    n_idx = idx.shape[0]
    ns = n_idx // ch
    return pl.pallas_call(
        functools.partial(_rowcopy_kernel, ch=ch, indexed_src=indexed_src),
        out_shape=jax.ShapeDtypeStruct((n_dst, src.shape[1]), src.dtype),
        grid=(ns,),
        in_specs=[pl.BlockSpec(memory_space=pl.ANY), pl.BlockSpec(memory_space=pl.ANY)],
        out_specs=pl.BlockSpec(memory_space=pl.ANY),
        scratch_shapes=[pltpu.SMEM((2, ch), I32), pltpu.SemaphoreType.DMA((2,)), pltpu.SemaphoreType.DMA],
        compiler_params=_cparams(("arbitrary",), 16),
        name="row_gather" if indexed_src else "row_scatter",
    )(idx.reshape(ns, ch), src)


def _expert_kernel(be_ref, nv_ref, x_ref, w1_ref, b1_ref, w2_ref, b2_ref, o_ref):
    b = pl.program_id(0)
    F = w2_ref.shape[1]
    half = x_ref.shape[1]

    @pl.when(nv_ref[b] > 0)
    def _():
        lo, hi = _unpack_halves(x_ref[...])
        a = (jnp.dot(lo, w1_ref[0, :half, :], preferred_element_type=F32)
             + jnp.dot(hi, w1_ref[0, half:, :], preferred_element_type=F32) + b1_ref[0])
        g = jnp.minimum(a[:, :F], SWIGLU_LIMIT)
        lin = jnp.clip(a[:, F:], -SWIGLU_LIMIT, SWIGLU_LIMIT)
        act = g * jax.nn.sigmoid(SWIGLU_ALPHA * g) * (lin + 1.0)
        y = jnp.dot(act.astype(BF16), w2_ref[0], preferred_element_type=F32) + b2_ref[0]
        o_ref[...] = _pack_halves(y)

    @pl.when(nv_ref[b] == 0)
    def _():
        o_ref[...] = jnp.zeros_like(o_ref)


def _experts(blk_e, blk_nv, xs, w1, b1, w2, b2, *, tm):
    rows, half = xs.shape
    E, D, F2 = w1.shape
    F = F2 // 2
    return pl.pallas_call(
        _expert_kernel,
        out_shape=jax.ShapeDtypeStruct((rows, half), U32),
        grid_spec=pltpu.PrefetchScalarGridSpec(
            num_scalar_prefetch=2, grid=(rows // tm,),
            in_specs=[pl.BlockSpec((tm, half), lambda b, be, nv: (b, 0)),
                      pl.BlockSpec((1, D, F2), lambda b, be, nv: (be[b], 0, 0)),
                      pl.BlockSpec((1, 1, F2), lambda b, be, nv: (be[b], 0, 0)),
                      pl.BlockSpec((1, F, D), lambda b, be, nv: (be[b], 0, 0)),
                      pl.BlockSpec((1, 1, D), lambda b, be, nv: (be[b], 0, 0))],
            out_specs=pl.BlockSpec((tm, half), lambda b, be, nv: (b, 0))),
        compiler_params=_cparams(("arbitrary",), 56),
        name="experts",
    )(blk_e, blk_nv, xs, w1, b1.reshape(E, 1, F2), w2, b2.reshape(E, 1, D))


def _moe(hpk, top_idx, w1, b1, w2, b2, *, tm=256):
    N = hpk.shape[0]
    blk_e, blk_nv, row_tok, row_slot = _moe_plan(top_idx, N, tm)
    xs = _rowcopy(row_tok.reshape(-1), hpk, row_tok.size, indexed_src=True)
    ys = _experts(blk_e, blk_nv, xs, w1, b1, w2, b2, tm=tm)
    return _rowcopy(row_slot.reshape(-1), ys, TOP_K * N, indexed_src=False)


def kernel(x, c, w_ada, b_ada, ada_table, norm_pre, norm_post, w_in, w_out, conv_w, conv_b, conv_ln_g, conv_ln_b,
           diff_lambda, diff_norm_g, w_router, b_router, w1, b1, w2, b2):
    B, S, D = x.shape
    depth = w_in.shape[0]
    N = B * S
    G = D // N_GROUPS
    H = G // HEAD_DIM
    E = w_router.shape[2]
    assert E == N_EXPERTS and S % MOBA_BLOCK == 0 and G % HEAD_DIM == 0
    nkb = S // MOBA_BLOCK

    cond = _ada(c, w_ada, b_ada)
    xf = x.reshape(N, D)
    h = None
    for l in range(depth):
        lam_init = 0.8 - 0.6 * math.exp(-0.3 * l)
        tab = ada_table[l]
        if l == 0:
            (h,) = _post(xf, cond, tab, tab, S=S, mode="none", nxt="plain", g_pre=norm_pre[l, 0], i_shift=0, i_scale=1)
        p = _matmul([h], w_in[l].astype(BF16), BF16, 512, 1024 if w_in.shape[2] % 1024 == 0 else G)

        y_conv = _conv_mixer(p, conv_w[l], conv_b[l], conv_ln_g[l], conv_ln_b[l], B=B, S=S, G=G)
        t = MOBA_BLOCK
        y_sb = _attention(functools.partial(_sb_kernel, t=t), p, [], [], [], B=B, S=S, G=G,
                          c_q=2 * H, c_k=3 * H, c_v=4 * H, t=t, name="stick_breaking")
        y_diff = _attention(
            functools.partial(_diff_kernel, t=t, lam_init=lam_init), p,
            [diff_lambda[l], diff_norm_g[l].reshape(1, HEAD_DIM)],
            [pl.BlockSpec((4, DIFF_QK_DIM), lambda b, h_, i: (0, 0)), pl.BlockSpec((1, HEAD_DIM), lambda b, h_, i: (0, 0))],
            [], B=B, S=S, G=G, c_q=5 * H, c_k=6 * H, c_v=7 * H, t=t, name="diff_attention")
        y_moba = _attention(
            functools.partial(_moba_kernel, t=t, nkb=nkb), p, [], [],
            [pltpu.VMEM((LANES, HEAD_DIM), BF16), pltpu.VMEM((LANES, HEAD_DIM), BF16)],
            B=B, S=S, G=G, c_q=8 * H, c_k=9 * H, c_v=10 * H, t=t, name="moba")

        y = _matmul([y_conv, y_sb, y_diff, y_moba], w_out[l].astype(BF16), F32, 512, 1024 if D % 1024 == 0 else D)

        wr = jnp.zeros((D, LANES), F32).at[:, :E].set(w_router[l])
        wr_hi = wr.astype(BF16)
        wr_lo = (wr - wr_hi.astype(F32)).astype(BF16)
        br = jnp.zeros((1, LANES), F32).at[0, :E].set(b_router[l])
        xf, hpk, top_idx, gates = _post(xf, cond, tab, tab, S=S, mode="dense", nxt="router", y=y, g_post=norm_post[l, 0],
                                        g_pre=norm_pre[l, 1], router=(wr_hi, wr_lo, br), i_gate=2, i_shift=3, i_scale=4)

        y4 = _moe(hpk, top_idx[:, :TOP_K], w1[l].astype(BF16), b1[l], w2[l].astype(BF16), b2[l])
        y4 = y4.reshape(TOP_K, N, D // 2)
        if l + 1 < depth:
            xf, h = _post(xf, cond, tab, ada_table[l + 1], S=S, mode="moe", nxt="plain", y4=y4, g4=gates,
                          g_post=norm_post[l, 1], g_pre=norm_pre[l + 1, 0], i_gate=5, i_shift=0, i_scale=1)
        else:
            (xf,) = _post(xf, cond, tab, tab, S=S, mode="moe", nxt="none", y4=y4, g4=gates, g_post=norm_post[l, 1], i_gate=5)
    return xf.reshape(B, S, D)
```

```python
import functools
import math

import jax
import jax.numpy as jnp
from jax import lax
from jax.experimental import pallas as pl
from jax.experimental.pallas import tpu as pltpu

F32 = jnp.float32
BF16 = jnp.bfloat16
U32 = jnp.uint32
I32 = jnp.int32

N_GROUPS = 4
HEAD_DIM = 128
DIFF_QK_DIM = HEAD_DIM // 2
CONV_WIDTH = 31
MOBA_BLOCK = 256
MOBA_TOPK = 3
N_EXPERTS = 32
TOP_K = 4
SWIGLU_ALPHA = 1.702
SWIGLU_LIMIT = 7.0
RMS_EPS = 1e-6
LN_EPS = 1e-5
N_MOD = 6
LANES = 128
CONV_HALO = 32
NEG_BIG = -1e30
MIB = 1024 * 1024


def _cparams(sem, vmem_mib):
    return pltpu.CompilerParams(dimension_semantics=sem, vmem_limit_bytes=vmem_mib * MIB)


def _rms(xf, g):
    return xf * lax.rsqrt(jnp.mean(xf * xf, axis=-1, keepdims=True) + RMS_EPS) * g


def _pack_halves(a):
    k = a.shape[1] // 2
    lo = lax.bitcast_convert_type(a[:, :k].astype(BF16).astype(F32), U32)
    hi = lax.bitcast_convert_type(a[:, k:].astype(BF16).astype(F32), U32)
    return (lo >> 16) | (hi & jnp.uint32(0xFFFF0000))


def _unpack_halves(w):
    lo = lax.bitcast_convert_type(w << 16, F32).astype(BF16)
    hi = lax.bitcast_convert_type(w & jnp.uint32(0xFFFF0000), F32).astype(BF16)
    return lo, hi


def _store_token_rows(ref, words):
    m, rpt = words.shape[0], words.shape[1] // LANES
    for c in range(rpt):
        ref[pl.ds(c, m, stride=rpt), :] = words[:, c * LANES:(c + 1) * LANES]


def _load_token_rows(ref, m):
    rpt = ref.shape[0] // m
    return jnp.concatenate([ref[pl.ds(c, m, stride=rpt), :] for c in range(rpt)], axis=1)


def _ada_kernel(c_ref, w_ref, b_ref, o_ref):
    c = c_ref[...]
    a = (c * jax.nn.sigmoid(c)).astype(BF16)
    o_ref[...] = jnp.dot(a, w_ref[...].astype(BF16), preferred_element_type=F32) + b_ref[...]


def _ada(c, w_ada, b_ada):
    B, D = c.shape
    n_out = w_ada.shape[1]
    tn = 512
    rows = 8
    c8 = jnp.zeros((rows, D), F32).at[:B].set(c)
    out = pl.pallas_call(
        _ada_kernel,
        out_shape=jax.ShapeDtypeStruct((rows, n_out), F32),
        grid=(n_out // tn,),
        in_specs=[pl.BlockSpec((rows, D), lambda j: (0, 0)),
                  pl.BlockSpec((D, tn), lambda j: (0, j)),
                  pl.BlockSpec((1, tn), lambda j: (0, j))],
        out_specs=pl.BlockSpec((rows, tn), lambda j: (0, j)),
        compiler_params=_cparams(("arbitrary",), 40),
        name="ada",
    )(c8, w_ada, b_ada.reshape(1, n_out))
    return out[:B].reshape(B, N_MOD, D)


def _router_topk(hf, wr_hi_ref, wr_lo_ref, br_ref):
    h_hi = hf.astype(BF16)
    h_lo = (hf - h_hi.astype(F32)).astype(BF16)
    logits = (jnp.dot(h_hi, wr_hi_ref[...], preferred_element_type=F32)
              + jnp.dot(h_lo, wr_hi_ref[...], preferred_element_type=F32)
              + jnp.dot(h_hi, wr_lo_ref[...], preferred_element_type=F32)) + br_ref[...]
    lane = lax.broadcasted_iota(I32, logits.shape, 1)
    lane_f = lane.astype(F32)
    l = jnp.where(lane < N_EXPERTS, logits, -jnp.inf)
    idx_out = jnp.zeros(logits.shape, F32)
    val_out = jnp.full(logits.shape, -jnp.inf, F32)
    m0 = None
    for k in range(TOP_K):
        m = jnp.max(l, axis=1, keepdims=True)
        am = jnp.min(jnp.where(l == m, lane_f, float(LANES)), axis=1, keepdims=True)
        idx_out = jnp.where(lane == k, am, idx_out)
        val_out = jnp.where(lane == k, m, val_out)
        l = jnp.where(lane_f == am, -jnp.inf, l)
        if k == 0:
            m0 = m
    e = jnp.where(lane < TOP_K, jnp.exp(val_out - m0), 0.0)
    gates = e / jnp.sum(e, axis=1, keepdims=True)
    return idx_out.astype(I32), gates


def _post_kernel(*refs, mode, nxt, i_gate, i_shift, i_scale):
    it = iter(refs)
    x_ref = next(it)
    if mode == "dense":
        y_ref = next(it)
    elif mode == "moe":
        y4_ref = next(it)
        g4_ref = next(it)
    cond_ref = next(it)
    tab_ref = next(it)
    tabn_ref = next(it)
    gpost_ref = next(it) if mode != "none" else None
    gpre_ref = next(it) if nxt != "none" else None
    if nxt == "router":
        wr_hi_ref, wr_lo_ref, br_ref = next(it), next(it), next(it)
    outs = list(it)

    mod = cond_ref[0] + tab_ref[...]
    modn = cond_ref[0] + tabn_ref[...]
    x = x_ref[...]
    if mode == "dense":
        y = y_ref[...].astype(F32)
    elif mode == "moe":
        g4 = g4_ref[...]
        y = None
        for k in range(TOP_K):
            lo, hi = _unpack_halves(_load_token_rows(y4_ref.at[k], x.shape[0]))
            yk = jnp.concatenate([lo.astype(F32), hi.astype(F32)], axis=1) * g4[:, k:k + 1]
            y = yk if y is None else y + yk
    o = 0
    if mode != "none":
        x = x + mod[i_gate:i_gate + 1, :] * _rms(y, gpost_ref[...])
        outs[o][...] = x
        o += 1
    if nxt != "none":
        hf = _rms(x, gpre_ref[...]) * (1.0 + modn[i_scale:i_scale + 1, :]) + modn[i_shift:i_shift + 1, :]
        if nxt == "plain":
            outs[o][...] = hf.astype(BF16)
        else:
            _store_token_rows(outs[o], _pack_halves(hf))
            idx, gates = _router_topk(hf, wr_hi_ref, wr_lo_ref, br_ref)
            outs[o + 1][...] = idx
            outs[o + 2][...] = gates


def _post(x, cond, tab, tab_next, *, S, mode, nxt, y=None, y4=None, g4=None, g_post=None, g_pre=None,
          router=None, i_gate=0, i_shift=0, i_scale=0, tm=256):
    N, D = x.shape
    nb = S // tm
    rpt = D // 2 // LANES
    row = lambda i: (i, 0)
    args, specs = [x], [pl.BlockSpec((tm, D), row)]
    if mode == "dense":
        args.append(y)
        specs.append(pl.BlockSpec((tm, D), row))
    elif mode == "moe":
        args += [y4, g4]
        specs += [pl.BlockSpec((TOP_K, tm * rpt, LANES), lambda i: (0, i, 0)), pl.BlockSpec((tm, LANES), row)]
    args += [cond, tab, tab_next]
    specs += [pl.BlockSpec((1, N_MOD, D), lambda i: (i // nb, 0, 0)), pl.BlockSpec((N_MOD, D), lambda i: (0, 0)),
              pl.BlockSpec((N_MOD, D), lambda i: (0, 0))]
    if mode != "none":
        args.append(g_post.reshape(1, D))
        specs.append(pl.BlockSpec((1, D), lambda i: (0, 0)))
    if nxt != "none":
        args.append(g_pre.reshape(1, D))
        specs.append(pl.BlockSpec((1, D), lambda i: (0, 0)))
    if nxt == "router":
        args += list(router)
        specs += [pl.BlockSpec((D, LANES), lambda i: (0, 0)), pl.BlockSpec((D, LANES), lambda i: (0, 0)),
                  pl.BlockSpec((1, LANES), lambda i: (0, 0))]
    out_shape, out_specs = [], []
    if mode != "none":
        out_shape.append(jax.ShapeDtypeStruct((N, D), F32))
        out_specs.append(pl.BlockSpec((tm, D), row))
    if nxt == "plain":
        out_shape.append(jax.ShapeDtypeStruct((N, D), BF16))
        out_specs.append(pl.BlockSpec((tm, D), row))
    elif nxt == "router":
        out_shape += [jax.ShapeDtypeStruct((N * rpt, LANES), U32), jax.ShapeDtypeStruct((N, LANES), I32),
                      jax.ShapeDtypeStruct((N, LANES), F32)]
        out_specs += [pl.BlockSpec((tm * rpt, LANES), row), pl.BlockSpec((tm, LANES), row), pl.BlockSpec((tm, LANES), row)]
    return pl.pallas_call(
        functools.partial(_post_kernel, mode=mode, nxt=nxt, i_gate=i_gate, i_shift=i_shift, i_scale=i_scale),
        out_shape=out_shape, grid=(N // tm,), in_specs=specs, out_specs=out_specs,
        compiler_params=_cparams(("arbitrary",), 56),
        name=f"post_{mode}_{nxt}",
    )(*args)


def _mm_kernel(*refs, n_a):
    a_refs, w_ref, o_ref = refs[:n_a], refs[n_a], refs[n_a + 1]
    acc, off = None, 0
    for a in a_refs:
        kg = a.shape[1]
        part = jnp.dot(a[...], w_ref[off:off + kg, :], preferred_element_type=F32)
        acc = part if acc is None else acc + part
        off += kg
    o_ref[...] = acc.astype(o_ref.dtype)


def _matmul(a_list, w, out_dtype, tm, tn):
    N = a_list[0].shape[0]
    K, n_out = w.shape
    specs = [pl.BlockSpec((tm, a.shape[1]), lambda i, j: (i, 0)) for a in a_list]
    specs.append(pl.BlockSpec((K, tn), lambda i, j: (0, j)))
    return pl.pallas_call(
        functools.partial(_mm_kernel, n_a=len(a_list)),
        out_shape=jax.ShapeDtypeStruct((N, n_out), out_dtype),
        grid=(N // tm, n_out // tn), in_specs=specs,
        out_specs=pl.BlockSpec((tm, tn), lambda i, j: (i, j)),
        compiler_params=_cparams(("arbitrary", "arbitrary"), 56),
        name="matmul",
    )(*a_list, w)


def _conv_kernel(val_ref, gate_ref, pval_ref, pgate_ref, w_ref, b_ref, lg_ref, lb_ref, o_ref, ubuf, ybuf, *, ts):
    G = o_ref.shape[1]
    i = pl.program_id(1)
    v = val_ref[...].astype(F32)
    g = gate_ref[...].astype(F32)
    ubuf[CONV_HALO:CONV_HALO + ts, :] = v * jax.nn.sigmoid(g)
    pv = pval_ref[...].astype(F32)
    pg = pgate_ref[...].astype(F32)
    ubuf[0:CONV_HALO, :] = jnp.where(i > 0, pv * jax.nn.sigmoid(pg), 0.0)
    base = CONV_HALO - (CONV_WIDTH - 1)
    rs = 128

    def chan(c, carry):
        c0 = pl.multiple_of(c * LANES, LANES)
        for r0 in range(0, ts, rs):
            acc = jnp.zeros((rs, LANES), F32)
            for j in range(CONV_WIDTH):
                acc = acc + w_ref[j:j + 1, pl.ds(c0, LANES)] * ubuf[base + j + r0:base + j + r0 + rs, pl.ds(c0, LANES)]
            ybuf[r0:r0 + rs, pl.ds(c0, LANES)] = acc
        return carry

    lax.fori_loop(0, G // LANES, chan, 0)
    for r0 in range(0, ts, 64):
        y = ybuf[r0:r0 + 64, :] + b_ref[...]
        mu = jnp.mean(y, axis=-1, keepdims=True)
        d = y - mu
        var = jnp.mean(d * d, axis=-1, keepdims=True)
        z = d * lax.rsqrt(var + LN_EPS) * lg_ref[...] + lb_ref[...]
        o_ref[r0:r0 + 64, :] = (z * jax.nn.sigmoid(z)).astype(o_ref.dtype)


def _conv_mixer(p, conv_w, conv_b, ln_g, ln_b, *, B, S, G, ts=256):
    nb = S // ts
    hb = ts // CONV_HALO
    wpad = jnp.zeros((CONV_HALO, G), F32).at[:CONV_WIDTH].set(conv_w)
    cur = lambda col: pl.BlockSpec((ts, G), lambda b, i: (b * nb + i, col))
    prev = lambda col: pl.BlockSpec((CONV_HALO, G), lambda b, i: (jnp.maximum((b * nb + i) * hb - 1, 0), col))
    vec = pl.BlockSpec((1, G), lambda b, i: (0, 0))
    return pl.pallas_call(
        functools.partial(_conv_kernel, ts=ts),
        out_shape=jax.ShapeDtypeStruct((B * S, G), BF16),
        grid=(B, nb),
        in_specs=[cur(0), cur(1), prev(0), prev(1), pl.BlockSpec((CONV_HALO, G), lambda b, i: (0, 0)), vec, vec, vec],
        out_specs=pl.BlockSpec((ts, G), lambda b, i: (b * nb + i, 0)),
        scratch_shapes=[pltpu.VMEM((ts + CONV_HALO, G), F32), pltpu.VMEM((ts, G), F32)],
        compiler_params=_cparams(("arbitrary", "arbitrary"), 32),
        name="conv_mixer",
    )(p, p, p, p, wpad, conv_b.reshape(1, G), ln_g.reshape(1, G), ln_b.reshape(1, G))


_NT = (((1,), (1,)), ((), ()))
HEADS_PER_STEP = 2
SB_UNDERFLOW = 110.0
DIFF_TILE = 512
MOBA_GROUP = 4


def _hs(hh):
    return slice(hh * HEAD_DIM, (hh + 1) * HEAD_DIM)


def _sb_kernel(q_ref, k_ref, v_ref, o_ref, *, t, nh):
    i = pl.program_id(2)
    qs = [q_ref[:, _hs(hh)] for hh in range(nh)]
    scale = HEAD_DIM ** -0.5
    row = lax.broadcasted_iota(I32, (t, t), 0)
    col = lax.broadcasted_iota(I32, (t, t), 1)
    tri = jnp.where(row >= col, 1.0, 0.0).astype(BF16)
    past = col < row

    def block(j, st, diag):
        j0 = pl.multiple_of(j * t, t)
        new = []
        for hh in range(nh):
            carry, acc = st[hh]
            kj = k_ref[pl.ds(j0, t), _hs(hh)]
            vj = v_ref[pl.ds(j0, t), _hs(hh)]
            z = lax.dot_general(qs[hh], kj, _NT, preferred_element_type=F32) * scale
            sp = jnp.maximum(z, 0.0) + jnp.log(1.0 + jnp.exp(-jnp.abs(z)))
            u = jnp.where(past, sp, 0.0) if diag else sp
            u_hi = u.astype(BF16)
            u_lo = (u - u_hi.astype(F32)).astype(BF16)
            tl = jnp.dot(u_hi, tri, preferred_element_type=F32) + jnp.dot(u_lo, tri, preferred_element_type=F32)
            e = z - (tl + carry)
            if diag:
                e = jnp.where(past, e, -jnp.inf)
            a = jnp.exp(e)
            acc = acc + jnp.dot(a.astype(BF16), vj, preferred_element_type=F32)
            new.append((carry + tl[:, 0:1], acc))
        return tuple(new)

    def live(st):
        lo = st[0][0]
        for hh in range(1, nh):
            lo = jnp.minimum(lo, st[hh][0])
        return (jnp.min(lo) < SB_UNDERFLOW).astype(I32)

    def body(c):
        st = block(i - 1 - c[0], c[2], False)
        return c[0] + 1, live(st), st

    init = tuple((jnp.zeros((t, 1), F32), jnp.zeros((t, HEAD_DIM), F32)) for _ in range(nh))
    st = block(i, init, True)
    _, _, st = lax.while_loop(lambda c: (c[0] < i) & (c[1] > 0), body, (jnp.int32(0), live(st), st))
    for hh in range(nh):
        o_ref[:, _hs(hh)] = st[hh][1].astype(o_ref.dtype)


def _diff_kernel(q_ref, k_ref, v_ref, dl_ref, g_ref, o_ref, *, t, nh, lam_init):
    i = pl.program_id(2)
    scale = DIFF_QK_DIM ** -0.5
    lane = lax.broadcasted_iota(I32, (t, HEAD_DIM), 1)
    qs = []
    for hh in range(nh):
        q = q_ref[:, _hs(hh)]
        zero = jnp.zeros_like(q)
        qs.append((jnp.where(lane < DIFF_QK_DIM, q, zero), jnp.where(lane >= DIFF_QK_DIM, q, zero)))
    row = lax.broadcasted_iota(I32, (t, t), 0)
    col = lax.broadcasted_iota(I32, (t, t), 1)
    causal = col <= row
    dl = dl_ref[...]
    lam = (jnp.exp(jnp.sum(dl[0:1] * dl[1:2], axis=1, keepdims=True))
           - jnp.exp(jnp.sum(dl[2:3] * dl[3:4], axis=1, keepdims=True)) + lam_init)

    def block(j, st, diag):
        j0 = pl.multiple_of(j * t, t)
        new = []
        for hh in range(nh):
            kj = k_ref[pl.ds(j0, t), _hs(hh)]
            vj = v_ref[pl.ds(j0, t), _hs(hh)]
            for mi in range(2):
                m, l, acc = st[2 * hh + mi]
                s = lax.dot_general(qs[hh][mi], kj, _NT, preferred_element_type=F32) * scale
                if diag:
                    s = jnp.where(causal, s, NEG_BIG)
                m_new = jnp.maximum(m, jnp.max(s, axis=1, keepdims=True))
                alpha = jnp.exp(m - m_new)
                pr = jnp.exp(s - m_new)
                l = alpha * l + jnp.sum(pr, axis=1, keepdims=True)
                acc = alpha * acc + jnp.dot(pr.astype(BF16), vj, preferred_element_type=F32)
                new.append((m_new, l, acc))
        return tuple(new)

    init = tuple((jnp.full((t, 1), NEG_BIG, F32), jnp.zeros((t, 1), F32), jnp.zeros((t, HEAD_DIM), F32))
                 for _ in range(2 * nh))
    st = block(i, init, True)
    st = lax.fori_loop(0, i, lambda j, s: block(j, s, False), st)
    for hh in range(nh):
        (_, l0, a0), (_, l1, a1) = st[2 * hh], st[2 * hh + 1]
        o = a0 / l0 - lam * (a1 / l1)
        o_ref[:, _hs(hh)] = (_rms(o, g_ref[...]) * (1.0 - lam_init)).astype(o_ref.dtype)


def _moba_kernel(q_ref, k_ref, v_ref, o_ref, km_hi, km_lo, *, t, nh, nkb, grp):
    own = pl.program_id(2)
    scale = HEAD_DIM ** -0.5
    topk = min(MOBA_TOPK, nkb)

    @pl.when(own == 0)
    def _():
        km_hi[...] = jnp.zeros_like(km_hi)
        km_lo[...] = jnp.zeros_like(km_lo)
        for hh in range(nh):
            for blk in range(nkb):
                km = jnp.mean(k_ref[blk * t:(blk + 1) * t, _hs(hh)].astype(F32), axis=0, keepdims=True)
                hi = km.astype(BF16)
                km_hi[hh, blk:blk + 1, :] = hi
                km_lo[hh, blk:blk + 1, :] = (km - hi.astype(F32)).astype(BF16)

    lane = lax.broadcasted_iota(I32, (t, LANES), 1)
    row = lax.broadcasted_iota(I32, (t, t), 0)
    col = lax.broadcasted_iota(I32, (t, t), 1)
    qs, sels = [], []
    for hh in range(nh):
        q = q_ref[:, _hs(hh)]
        gate = (lax.dot_general(q, km_hi[hh], _NT, preferred_element_type=F32)
                + lax.dot_general(q, km_lo[hh], _NT, preferred_element_type=F32))
        rank = jnp.zeros(gate.shape, F32)
        for jb in range(nkb):
            c = gate[:, jb:jb + 1]
            beats = (c > gate) | ((c == gate) & (lane > jb))
            rank = rank + jnp.where(beats, jnp.where(jb < own, 1.0, 0.0), 0.0)
        qs.append(q)
        sels.append(jnp.where((rank < float(topk)) & (lane < own), 1.0, 0.0))

    def attend(j, nblk, st, masks):
        j0 = pl.multiple_of(j * t, t)
        new = []
        for hh in range(nh):
            m, l, acc = st[hh]
            kj = k_ref[pl.ds(j0, nblk * t), _hs(hh)]
            vj = v_ref[pl.ds(j0, nblk * t), _hs(hh)]
            s = lax.dot_general(qs[hh], kj, _NT, preferred_element_type=F32) * scale
            s = jnp.concatenate([jnp.where(masks[hh][b], s[:, b * t:(b + 1) * t], NEG_BIG) for b in range(nblk)], axis=1)
            m_new = jnp.maximum(m, jnp.max(s, axis=1, keepdims=True))
            alpha = jnp.exp(m - m_new)
            pr = jnp.exp(s - m_new)
            l = alpha * l + jnp.sum(pr, axis=1, keepdims=True)
            acc = alpha * acc + jnp.dot(pr.astype(BF16), vj, preferred_element_type=F32)
            new.append((m_new, l, acc))
        return tuple(new)

    init = tuple((jnp.full((t, 1), NEG_BIG, F32), jnp.zeros((t, 1), F32), jnp.zeros((t, HEAD_DIM), F32))
                 for _ in range(nh))
    st = attend(own, 1, init, [[col <= row]] * nh)

    def body(g, st):
        picked = [[jnp.max(jnp.where(lane == g * grp + b, sels[hh], 0.0), axis=1, keepdims=True) > 0.5
                   for b in range(grp)] for hh in range(nh)]
        return attend(g * grp, grp, st, picked)

    st = lax.fori_loop(0, (own + grp - 1) // grp, body, st)
    for hh in range(nh):
        _, l, acc = st[hh]
        o_ref[:, _hs(hh)] = (acc / l).astype(o_ref.dtype)


def _attention(kernel_fn, p, extra, extra_specs, scratch, *, B, S, G, c_q, c_k, c_v, t, nh, name):
    H = G // HEAD_DIM
    nq = S // t
    w = nh * HEAD_DIM
    assert H % nh == 0 and c_q % nh == 0 and c_k % nh == 0 and c_v % nh == 0
    return pl.pallas_call(
        kernel_fn,
        out_shape=jax.ShapeDtypeStruct((B * S, G), BF16),
        grid=(B, H // nh, nq),
        in_specs=[pl.BlockSpec((t, w), lambda b, h, i: (b * nq + i, c_q // nh + h)),
                  pl.BlockSpec((S, w), lambda b, h, i: (b, c_k // nh + h)),
                  pl.BlockSpec((S, w), lambda b, h, i: (b, c_v // nh + h))] + extra_specs,
        out_specs=pl.BlockSpec((t, w), lambda b, h, i: (b * nq + i, h)),
        scratch_shapes=scratch,
        compiler_params=_cparams(("arbitrary", "arbitrary", "arbitrary"), 40),
        name=name,
    )(p, p, p, *extra)


def _moe_plan(top_idx, N, tm):
    E, NK = N_EXPERTS, N * TOP_K
    nblocks = (NK + E * (tm - 1) + tm - 1) // tm
    e_flat = top_idx.reshape(NK)
    order = jnp.argsort(e_flat, stable=True).astype(I32)
    counts = jnp.sum((e_flat[:, None] == jnp.arange(E, dtype=I32)[None, :]).astype(I32), axis=0)
    starts = jnp.cumsum(counts) - counts
    nblk_e = (counts + tm - 1) // tm
    bend = jnp.cumsum(nblk_e)
    bstart = bend - nblk_e
    b = jnp.arange(nblocks, dtype=I32)
    blk_e = jnp.minimum(jnp.sum((bend[None, :] <= b[:, None]).astype(I32), axis=1), E - 1).astype(I32)
    blk_off = (b - bstart[blk_e]) * tm
    blk_nv = jnp.where(b < bend[-1], jnp.clip(counts[blk_e] - blk_off, 0, tm), 0).astype(I32)
    ri = jnp.arange(tm, dtype=I32)
    valid = ri[None, :] < blk_nv[:, None]
    src = jnp.clip(starts[blk_e][:, None] + blk_off[:, None] + ri[None, :], 0, NK - 1)
    flat = order[src]
    row_tok = jnp.where(valid, flat // TOP_K, 0).astype(I32)
    row_slot = jnp.where(valid, (flat % TOP_K) * N + flat // TOP_K, -1).astype(I32)
    return blk_e, blk_nv, row_tok, row_slot


def _rowcopy_kernel(idx_hbm, src, dst, idx_smem, isem, csem, *, ch, rpt, gather):
    s = pl.program_id(0)
    ns = pl.num_programs(0)
    slot = s % 2

    def idx_copy(step, sl):
        return pltpu.make_async_copy(idx_hbm.at[step], idx_smem.at[sl], isem.at[sl])

    @pl.when(s == 0)
    def _():
        idx_copy(0, 0).start()

    idx_copy(s, slot).wait()

    @pl.when(s + 1 < ns)
    def _():
        idx_copy(s + 1, 1 - slot).start()

    def row_copy(r, t):
        blk = pl.ds(pl.multiple_of(r * rpt, rpt), rpt)
        far = pl.ds(pl.multiple_of(t * rpt, rpt), rpt)
        if gather:
            return pltpu.make_async_copy(src.at[far], dst.at[blk], csem)
        return pltpu.make_async_copy(src.at[blk], dst.at[far], csem)

    def issue(r, c):
        t = idx_smem[slot, r]

        @pl.when(t >= 0)
        def _():
            row_copy(r, t).start()
        return c

    def drain(r, c):
        t = idx_smem[slot, r]

        @pl.when(t >= 0)
        def _():
            row_copy(r, t).wait()
        return c

    lax.fori_loop(0, ch, issue, 0, unroll=8)
    lax.fori_loop(0, ch, drain, 0, unroll=8)


def _rowcopy(idx, src, n_dst_tokens, *, rpt, gather, ch=512):
    ns = idx.shape[0] // ch
    blk = pl.BlockSpec((ch * rpt, LANES), lambda s: (s, 0))
    hbm = pl.BlockSpec(memory_space=pl.ANY)
    return pl.pallas_call(
        functools.partial(_rowcopy_kernel, ch=ch, rpt=rpt, gather=gather),
        out_shape=jax.ShapeDtypeStruct((n_dst_tokens * rpt, LANES), src.dtype),
        grid=(ns,),
        in_specs=[hbm, hbm if gather else blk],
        out_specs=blk if gather else hbm,
        scratch_shapes=[pltpu.SMEM((2, ch), I32), pltpu.SemaphoreType.DMA((2,)), pltpu.SemaphoreType.DMA],
        compiler_params=_cparams(("arbitrary",), 32),
        name="row_gather" if gather else "row_scatter",
    )(idx.reshape(ns, ch), src)


def _expert_kernel(be_ref, nv_ref, x_ref, w1_ref, b1_ref, w2_ref, b2_ref, o_ref, *, tm):
    b = pl.program_id(0)
    F = w2_ref.shape[1]
    half = w1_ref.shape[1] // 2

    @pl.when(nv_ref[b] > 0)
    def _():
        lo, hi = _unpack_halves(_load_token_rows(x_ref, tm))
        a = (jnp.dot(lo, w1_ref[0, :half, :], preferred_element_type=F32)
             + jnp.dot(hi, w1_ref[0, half:, :], preferred_element_type=F32) + b1_ref[0])
        g = jnp.minimum(a[:, :F], SWIGLU_LIMIT)
        lin = jnp.clip(a[:, F:], -SWIGLU_LIMIT, SWIGLU_LIMIT)
        act = g * jax.nn.sigmoid(SWIGLU_ALPHA * g) * (lin + 1.0)
        y = jnp.dot(act.astype(BF16), w2_ref[0], preferred_element_type=F32) + b2_ref[0]
        _store_token_rows(o_ref, _pack_halves(y))

    @pl.when(nv_ref[b] == 0)
    def _():
        o_ref[...] = jnp.zeros_like(o_ref)


def _experts(blk_e, blk_nv, xs, w1, b1, w2, b2, *, tm, rpt):
    E, D, F2 = w1.shape
    F = F2 // 2
    nblocks = xs.shape[0] // (tm * rpt)
    return pl.pallas_call(
        functools.partial(_expert_kernel, tm=tm),
        out_shape=jax.ShapeDtypeStruct(xs.shape, U32),
        grid_spec=pltpu.PrefetchScalarGridSpec(
            num_scalar_prefetch=2, grid=(nblocks,),
            in_specs=[pl.BlockSpec((tm * rpt, LANES), lambda b, be, nv: (b, 0)),
                      pl.BlockSpec((1, D, F2), lambda b, be, nv: (be[b], 0, 0)),
                      pl.BlockSpec((1, 1, F2), lambda b, be, nv: (be[b], 0, 0)),
                      pl.BlockSpec((1, F, D), lambda b, be, nv: (be[b], 0, 0)),
                      pl.BlockSpec((1, 1, D), lambda b, be, nv: (be[b], 0, 0))],
            out_specs=pl.BlockSpec((tm * rpt, LANES), lambda b, be, nv: (b, 0))),
        compiler_params=_cparams(("arbitrary",), 56),
        name="experts",
    )(blk_e, blk_nv, xs, w1, b1.reshape(E, 1, F2), w2, b2.reshape(E, 1, D))


def _moe(hpk, top_idx, w1, b1, w2, b2, *, N, rpt, tm=256):
    blk_e, blk_nv, row_tok, row_slot = _moe_plan(top_idx, N, tm)
    xs = _rowcopy(row_tok.reshape(-1), hpk, row_tok.size, rpt=rpt, gather=True)
    ys = _experts(blk_e, blk_nv, xs, w1, b1, w2, b2, tm=tm, rpt=rpt)
    return _rowcopy(row_slot.reshape(-1), ys, TOP_K * N, rpt=rpt, gather=False)


def kernel(x, c, w_ada, b_ada, ada_table, norm_pre, norm_post, w_in, w_out, conv_w, conv_b, conv_ln_g, conv_ln_b,
           diff_lambda, diff_norm_g, w_router, b_router, w1, b1, w2, b2):
    B, S, D = x.shape
    depth = w_in.shape[0]
    N = B * S
    G = D // N_GROUPS
    H = G // HEAD_DIM
    E = w_router.shape[2]
    assert E == N_EXPERTS and S % MOBA_BLOCK == 0 and G % HEAD_DIM == 0
    nkb = S // MOBA_BLOCK

    cond = _ada(c, w_ada, b_ada)
    xf = x.reshape(N, D)
    h = None
    for l in range(depth):
        lam_init = 0.8 - 0.6 * math.exp(-0.3 * l)
        tab = ada_table[l]
        if l == 0:
            (h,) = _post(xf, cond, tab, tab, S=S, mode="none", nxt="plain", g_pre=norm_pre[l, 0], i_shift=0, i_scale=1)
        p = _matmul([h], w_in[l].astype(BF16), BF16, 512, 1024 if w_in.shape[2] % 1024 == 0 else G)

        y_conv = _conv_mixer(p, conv_w[l], conv_b[l], conv_ln_g[l], conv_ln_b[l], B=B, S=S, G=G)
        t = MOBA_BLOCK
        nh = HEADS_PER_STEP
        td = math.gcd(DIFF_TILE, S)
        y_sb = _attention(functools.partial(_sb_kernel, t=t, nh=nh), p, [], [], [], B=B, S=S, G=G,
                          c_q=2 * H, c_k=3 * H, c_v=4 * H, t=t, nh=nh, name="stick_breaking")
        y_diff = _attention(
            functools.partial(_diff_kernel, t=td, nh=nh, lam_init=lam_init), p,
            [diff_lambda[l], diff_norm_g[l].reshape(1, HEAD_DIM)],
            [pl.BlockSpec((4, DIFF_QK_DIM), lambda b, h_, i: (0, 0)), pl.BlockSpec((1, HEAD_DIM), lambda b, h_, i: (0, 0))],
            [], B=B, S=S, G=G, c_q=5 * H, c_k=6 * H, c_v=7 * H, t=td, nh=nh, name="diff_attention")
        y_moba = _attention(
            functools.partial(_moba_kernel, t=t, nh=nh, nkb=nkb, grp=math.gcd(MOBA_GROUP, nkb)), p, [], [],
            [pltpu.VMEM((nh, LANES, HEAD_DIM), BF16), pltpu.VMEM((nh, LANES, HEAD_DIM), BF16)],
            B=B, S=S, G=G, c_q=8 * H, c_k=9 * H, c_v=10 * H, t=t, nh=nh, name="moba")

        y = _matmul([y_conv, y_sb, y_diff, y_moba], w_out[l].astype(BF16), F32, 512, 1024 if D % 1024 == 0 else D)

        wr = jnp.zeros((D, LANES), F32).at[:, :E].set(w_router[l])
        wr_hi = wr.astype(BF16)
        wr_lo = (wr - wr_hi.astype(F32)).astype(BF16)
        br = jnp.zeros((1, LANES), F32).at[0, :E].set(b_router[l])
        xf, hpk, top_idx, gates = _post(xf, cond, tab, tab, S=S, mode="dense", nxt="router", y=y, g_post=norm_post[l, 0],
                                        g_pre=norm_pre[l, 1], router=(wr_hi, wr_lo, br), i_gate=2, i_shift=3, i_scale=4)

        rpt = D // 2 // LANES
        y4 = _moe(hpk, top_idx[:, :TOP_K], w1[l].astype(BF16), b1[l], w2[l].astype(BF16), b2[l], N=N, rpt=rpt)
        y4 = y4.reshape(TOP_K, N * rpt, LANES)
        if l + 1 < depth:
            xf, h = _post(xf, cond, tab, ada_table[l + 1], S=S, mode="moe", nxt="plain", y4=y4, g4=gates,
                          g_post=norm_post[l, 1], g_pre=norm_pre[l + 1, 0], i_gate=5, i_shift=0, i_scale=1)
        else:
            (xf,) = _post(xf, cond, tab, tab, S=S, mode="moe", nxt="none", y4=y4, g4=gates, g_post=norm_post[l, 1], i_gate=5)
    return xf.reshape(B, S, D)
```

```python
import functools
import math

import jax
import jax.numpy as jnp
from jax import lax
from jax.experimental import pallas as pl
from jax.experimental.pallas import tpu as pltpu

F32 = jnp.float32
BF16 = jnp.bfloat16
U32 = jnp.uint32
I32 = jnp.int32

N_GROUPS = 4
HEAD_DIM = 128
DIFF_QK_DIM = HEAD_DIM // 2
CONV_WIDTH = 31
MOBA_BLOCK = 256
MOBA_TOPK = 3
N_EXPERTS = 32
TOP_K = 4
SWIGLU_ALPHA = 1.702
SWIGLU_LIMIT = 7.0
RMS_EPS = 1e-6
LN_EPS = 1e-5
N_MOD = 6
LANES = 128
CONV_HALO = 32
NEG_BIG = -1e30
MIB = 1024 * 1024


def _cparams(sem, vmem_mib):
    return pltpu.CompilerParams(dimension_semantics=sem, vmem_limit_bytes=vmem_mib * MIB)


def _rms(xf, g):
    return xf * lax.rsqrt(jnp.mean(xf * xf, axis=-1, keepdims=True) + RMS_EPS) * g


def _pack_halves(a):
    k = a.shape[1] // 2
    lo = lax.bitcast_convert_type(a[:, :k].astype(BF16).astype(F32), U32)
    hi = lax.bitcast_convert_type(a[:, k:].astype(BF16).astype(F32), U32)
    return (lo >> 16) | (hi & jnp.uint32(0xFFFF0000))


def _unpack_halves(w):
    lo = lax.bitcast_convert_type(w << 16, F32).astype(BF16)
    hi = lax.bitcast_convert_type(w & jnp.uint32(0xFFFF0000), F32).astype(BF16)
    return lo, hi


def _store_token_rows(ref, words):
    m, rpt = words.shape[0], words.shape[1] // LANES
    for c in range(rpt):
        ref[pl.ds(c, m, stride=rpt), :] = words[:, c * LANES:(c + 1) * LANES]


def _load_token_rows(ref, m):
    rpt = ref.shape[0] // m
    return jnp.concatenate([ref[pl.ds(c, m, stride=rpt), :] for c in range(rpt)], axis=1)


def _ada_kernel(c_ref, w_ref, b_ref, o_ref):
    c = c_ref[...]
    a = (c * jax.nn.sigmoid(c)).astype(BF16)
    o_ref[...] = jnp.dot(a, w_ref[...].astype(BF16), preferred_element_type=F32) + b_ref[...]


def _ada(c, w_ada, b_ada):
    B, D = c.shape
    n_out = w_ada.shape[1]
    tn = 512
    rows = 8
    c8 = jnp.zeros((rows, D), F32).at[:B].set(c)
    out = pl.pallas_call(
        _ada_kernel,
        out_shape=jax.ShapeDtypeStruct((rows, n_out), F32),
        grid=(n_out // tn,),
        in_specs=[pl.BlockSpec((rows, D), lambda j: (0, 0)),
                  pl.BlockSpec((D, tn), lambda j: (0, j)),
                  pl.BlockSpec((1, tn), lambda j: (0, j))],
        out_specs=pl.BlockSpec((rows, tn), lambda j: (0, j)),
        compiler_params=_cparams(("arbitrary",), 40),
        name="ada",
    )(c8, w_ada, b_ada.reshape(1, n_out))
    return out[:B].reshape(B, N_MOD, D)


def _router_topk(hf, wr_hi_ref, wr_lo_ref, br_ref):
    h_hi = hf.astype(BF16)
    h_lo = (hf - h_hi.astype(F32)).astype(BF16)
    logits = (jnp.dot(h_hi, wr_hi_ref[...], preferred_element_type=F32)
              + jnp.dot(h_lo, wr_hi_ref[...], preferred_element_type=F32)
              + jnp.dot(h_hi, wr_lo_ref[...], preferred_element_type=F32)) + br_ref[...]
    lane = lax.broadcasted_iota(I32, logits.shape, 1)
    lane_f = lane.astype(F32)
    l = jnp.where(lane < N_EXPERTS, logits, -jnp.inf)
    idx_out = jnp.zeros(logits.shape, F32)
    val_out = jnp.full(logits.shape, -jnp.inf, F32)
    m0 = None
    for k in range(TOP_K):
        m = jnp.max(l, axis=1, keepdims=True)
        am = jnp.min(jnp.where(l == m, lane_f, float(LANES)), axis=1, keepdims=True)
        idx_out = jnp.where(lane == k, am, idx_out)
        val_out = jnp.where(lane == k, m, val_out)
        l = jnp.where(lane_f == am, -jnp.inf, l)
        if k == 0:
            m0 = m
    e = jnp.where(lane < TOP_K, jnp.exp(val_out - m0), 0.0)
    gates = e / jnp.sum(e, axis=1, keepdims=True)
    return idx_out.astype(I32), gates


def _post_kernel(*refs, mode, nxt, i_gate, i_shift, i_scale):
    it = iter(refs)
    x_ref = next(it)
    if mode == "dense":
        y_ref = next(it)
    elif mode == "moe":
        y4_ref = next(it)
        g4_ref = next(it)
    cond_ref = next(it)
    tab_ref = next(it)
    tabn_ref = next(it)
    gpost_ref = next(it) if mode != "none" else None
    gpre_ref = next(it) if nxt != "none" else None
    if nxt == "router":
        wr_hi_ref, wr_lo_ref, br_ref = next(it), next(it), next(it)
    outs = list(it)

    mod = cond_ref[0] + tab_ref[...]
    modn = cond_ref[0] + tabn_ref[...]
    x = x_ref[...]
    if mode == "dense":
        y = y_ref[...].astype(F32)
    elif mode == "moe":
        g4 = g4_ref[...]
        y = None
        for k in range(TOP_K):
            lo, hi = _unpack_halves(_load_token_rows(y4_ref.at[k], x.shape[0]))
            yk = jnp.concatenate([lo.astype(F32), hi.astype(F32)], axis=1) * g4[:, k:k + 1]
            y = yk if y is None else y + yk
    o = 0
    if mode != "none":
        x = x + mod[i_gate:i_gate + 1, :] * _rms(y, gpost_ref[...])
        outs[o][...] = x
        o += 1
    if nxt != "none":
        hf = _rms(x, gpre_ref[...]) * (1.0 + modn[i_scale:i_scale + 1, :]) + modn[i_shift:i_shift + 1, :]
        if nxt == "plain":
            outs[o][...] = hf.astype(BF16)
        else:
            _store_token_rows(outs[o], _pack_halves(hf))
            idx, gates = _router_topk(hf, wr_hi_ref, wr_lo_ref, br_ref)
            outs[o + 1][...] = idx
            outs[o + 2][...] = gates


def _post(x, cond, tab, tab_next, *, S, mode, nxt, y=None, y4=None, g4=None, g_post=None, g_pre=None,
          router=None, i_gate=0, i_shift=0, i_scale=0, tm=256):
    N, D = x.shape
    nb = S // tm
    rpt = D // 2 // LANES
    row = lambda i: (i, 0)
    args, specs = [x], [pl.BlockSpec((tm, D), row)]
    if mode == "dense":
        args.append(y)
        specs.append(pl.BlockSpec((tm, D), row))
    elif mode == "moe":
        args += [y4, g4]
        specs += [pl.BlockSpec((TOP_K, tm * rpt, LANES), lambda i: (0, i, 0)), pl.BlockSpec((tm, LANES), row)]
    args += [cond, tab, tab_next]
    specs += [pl.BlockSpec((1, N_MOD, D), lambda i: (i // nb, 0, 0)), pl.BlockSpec((N_MOD, D), lambda i: (0, 0)),
              pl.BlockSpec((N_MOD, D), lambda i: (0, 0))]
    if mode != "none":
        args.append(g_post.reshape(1, D))
        specs.append(pl.BlockSpec((1, D), lambda i: (0, 0)))
    if nxt != "none":
        args.append(g_pre.reshape(1, D))
        specs.append(pl.BlockSpec((1, D), lambda i: (0, 0)))
    if nxt == "router":
        args += list(router)
        specs += [pl.BlockSpec((D, LANES), lambda i: (0, 0)), pl.BlockSpec((D, LANES), lambda i: (0, 0)),
                  pl.BlockSpec((1, LANES), lambda i: (0, 0))]
    out_shape, out_specs = [], []
    if mode != "none":
        out_shape.append(jax.ShapeDtypeStruct((N, D), F32))
        out_specs.append(pl.BlockSpec((tm, D), row))
    if nxt == "plain":
        out_shape.append(jax.ShapeDtypeStruct((N, D), BF16))
        out_specs.append(pl.BlockSpec((tm, D), row))
    elif nxt == "router":
        out_shape += [jax.ShapeDtypeStruct((N * rpt, LANES), U32), jax.ShapeDtypeStruct((N, LANES), I32),
                      jax.ShapeDtypeStruct((N, LANES), F32)]
        out_specs += [pl.BlockSpec((tm * rpt, LANES), row), pl.BlockSpec((tm, LANES), row), pl.BlockSpec((tm, LANES), row)]
    return pl.pallas_call(
        functools.partial(_post_kernel, mode=mode, nxt=nxt, i_gate=i_gate, i_shift=i_shift, i_scale=i_scale),
        out_shape=out_shape, grid=(N // tm,), in_specs=specs, out_specs=out_specs,
        compiler_params=_cparams(("arbitrary",), 56),
        name=f"post_{mode}_{nxt}",
    )(*args)


def _mm_kernel(*refs, n_a):
    a_refs, w_ref, o_ref = refs[:n_a], refs[n_a], refs[n_a + 1]
    acc, off = None, 0
    for a in a_refs:
        kg = a.shape[1]
        part = jnp.dot(a[...], w_ref[off:off + kg, :], preferred_element_type=F32)
        acc = part if acc is None else acc + part
        off += kg
    o_ref[...] = acc.astype(o_ref.dtype)


def _matmul(a_list, w, out_dtype, tm, tn):
    N = a_list[0].shape[0]
    K, n_out = w.shape
    specs = [pl.BlockSpec((tm, a.shape[1]), lambda i, j: (i, 0)) for a in a_list]
    specs.append(pl.BlockSpec((K, tn), lambda i, j: (0, j)))
    return pl.pallas_call(
        functools.partial(_mm_kernel, n_a=len(a_list)),
        out_shape=jax.ShapeDtypeStruct((N, n_out), out_dtype),
        grid=(N // tm, n_out // tn), in_specs=specs,
        out_specs=pl.BlockSpec((tm, tn), lambda i, j: (i, j)),
        compiler_params=_cparams(("arbitrary", "arbitrary"), 56),
        name="matmul",
    )(*a_list, w)


def _cast_kernel(x_ref, o_ref):
    o_ref[...] = x_ref[0].astype(o_ref.dtype)


def _layer_bf16(w, l):
    C = w.shape[-1]
    R = math.prod(w.shape[1:-1])
    tr = math.gcd(R, max(8, pl.next_power_of_2(CAST_BLOCK_BYTES // (4 * C) + 1) // 2))
    out = pl.pallas_call(
        _cast_kernel,
        out_shape=jax.ShapeDtypeStruct((R, C), BF16),
        grid=(R // tr,),
        in_specs=[pl.BlockSpec((1, tr, C), lambda i: (l, i, 0))],
        out_specs=pl.BlockSpec((tr, C), lambda i: (i, 0)),
        compiler_params=_cparams(("arbitrary",), 40),
        name="cast_bf16",
    )(w.reshape(w.shape[0], R, C))
    return out.reshape(w.shape[1:])


def _conv_kernel(val_ref, gate_ref, pval_ref, pgate_ref, w_ref, b_ref, lg_ref, lb_ref, o_ref, ubuf, ybuf, *, ts):
    G = o_ref.shape[1]
    i = pl.program_id(1)
    v = val_ref[...].astype(F32)
    g = gate_ref[...].astype(F32)
    ubuf[CONV_HALO:CONV_HALO + ts, :] = v * jax.nn.sigmoid(g)
    pv = pval_ref[...].astype(F32)
    pg = pgate_ref[...].astype(F32)
    ubuf[0:CONV_HALO, :] = jnp.where(i > 0, pv * jax.nn.sigmoid(pg), 0.0)
    base = CONV_HALO - (CONV_WIDTH - 1)
    rs = 128

    def chan(c, carry):
        c0 = pl.multiple_of(c * LANES, LANES)
        for r0 in range(0, ts, rs):
            acc = jnp.zeros((rs, LANES), F32)
            for j in range(CONV_WIDTH):
                acc = acc + w_ref[j:j + 1, pl.ds(c0, LANES)] * ubuf[base + j + r0:base + j + r0 + rs, pl.ds(c0, LANES)]
            ybuf[r0:r0 + rs, pl.ds(c0, LANES)] = acc
        return carry

    lax.fori_loop(0, G // LANES, chan, 0)
    for r0 in range(0, ts, 64):
        y = ybuf[r0:r0 + 64, :] + b_ref[...]
        mu = jnp.mean(y, axis=-1, keepdims=True)
        d = y - mu
        var = jnp.mean(d * d, axis=-1, keepdims=True)
        z = d * lax.rsqrt(var + LN_EPS) * lg_ref[...] + lb_ref[...]
        o_ref[r0:r0 + 64, :] = (z * jax.nn.sigmoid(z)).astype(o_ref.dtype)


def _conv_mixer(p, conv_w, conv_b, ln_g, ln_b, *, B, S, G, ts=256):
    nb = S // ts
    hb = ts // CONV_HALO
    wpad = jnp.zeros((CONV_HALO, G), F32).at[:CONV_WIDTH].set(conv_w)
    cur = lambda col: pl.BlockSpec((ts, G), lambda b, i: (b * nb + i, col))
    prev = lambda col: pl.BlockSpec((CONV_HALO, G), lambda b, i: (jnp.maximum((b * nb + i) * hb - 1, 0), col))
    vec = pl.BlockSpec((1, G), lambda b, i: (0, 0))
    return pl.pallas_call(
        functools.partial(_conv_kernel, ts=ts),
        out_shape=jax.ShapeDtypeStruct((B * S, G), BF16),
        grid=(B, nb),
        in_specs=[cur(0), cur(1), prev(0), prev(1), pl.BlockSpec((CONV_HALO, G), lambda b, i: (0, 0)), vec, vec, vec],
        out_specs=pl.BlockSpec((ts, G), lambda b, i: (b * nb + i, 0)),
        scratch_shapes=[pltpu.VMEM((ts + CONV_HALO, G), F32), pltpu.VMEM((ts, G), F32)],
        compiler_params=_cparams(("arbitrary", "arbitrary"), 32),
        name="conv_mixer",
    )(p, p, p, p, wpad, conv_b.reshape(1, G), ln_g.reshape(1, G), ln_b.reshape(1, G))


_NT = (((1,), (1,)), ((), ()))
HEADS_PER_STEP = 2
SB_UNDERFLOW = 110.0
DIFF_TILE = 512
MOBA_GROUP = 4
CAST_BLOCK_BYTES = 8 * MIB
WAIT_GROUP = 64
LOG2E = 1.4426950408889634


def _hs(hh):
    return slice(hh * HEAD_DIM, (hh + 1) * HEAD_DIM)


def _sb_kernel(q_ref, k_ref, v_ref, o_ref, *, t, nh):
    i = pl.program_id(2)
    qs = [q_ref[:, _hs(hh)] for hh in range(nh)]
    scale = HEAD_DIM ** -0.5
    row = lax.broadcasted_iota(I32, (t, t), 0)
    col = lax.broadcasted_iota(I32, (t, t), 1)
    tri = jnp.where(row >= col, 1.0, 0.0).astype(BF16)
    past = col < row

    def block(j, st, diag):
        j0 = pl.multiple_of(j * t, t)
        new = []
        for hh in range(nh):
            carry, acc = st[hh]
            kj = k_ref[pl.ds(j0, t), _hs(hh)]
            vj = v_ref[pl.ds(j0, t), _hs(hh)]
            z = lax.dot_general(qs[hh], kj, _NT, preferred_element_type=F32) * scale
            sp = jnp.maximum(z, 0.0) + jnp.log(1.0 + jnp.exp(-jnp.abs(z)))
            u = jnp.where(past, sp, 0.0) if diag else sp
            u_hi = u.astype(BF16)
            u_lo = (u - u_hi.astype(F32)).astype(BF16)
            tl = jnp.dot(u_hi, tri, preferred_element_type=F32) + jnp.dot(u_lo, tri, preferred_element_type=F32)
            e = z - (tl + carry)
            if diag:
                e = jnp.where(past, e, -jnp.inf)
            a = jnp.exp(e)
            acc = acc + jnp.dot(a.astype(BF16), vj, preferred_element_type=F32)
            new.append((carry + tl[:, 0:1], acc))
        return tuple(new)

    def live(st):
        lo = st[0][0]
        for hh in range(1, nh):
            lo = jnp.minimum(lo, st[hh][0])
        return (jnp.min(lo) < SB_UNDERFLOW).astype(I32)

    def body(c):
        st = block(i - 1 - c[0], c[2], False)
        return c[0] + 1, live(st), st

    init = tuple((jnp.zeros((t, 1), F32), jnp.zeros((t, HEAD_DIM), F32)) for _ in range(nh))
    st = block(i, init, True)
    _, _, st = lax.while_loop(lambda c: (c[0] < i) & (c[1] > 0), body, (jnp.int32(0), live(st), st))
    for hh in range(nh):
        o_ref[:, _hs(hh)] = st[hh][1].astype(o_ref.dtype)


def _diff_kernel(q_ref, k_ref, v_ref, dl_ref, g_ref, o_ref, *, t, nh, lam_init):
    i = pl.program_id(2)
    scale = DIFF_QK_DIM ** -0.5
    assert math.frexp(scale)[0] == 0.5
    lane = lax.broadcasted_iota(I32, (t, HEAD_DIM), 1)
    qs = []
    for hh in range(nh):
        q = q_ref[:, _hs(hh)] * scale
        zero = jnp.zeros_like(q)
        qs.append((jnp.where(lane < DIFF_QK_DIM, q, zero), jnp.where(lane >= DIFF_QK_DIM, q, zero)))
    row = lax.broadcasted_iota(I32, (t, t), 0)
    col = lax.broadcasted_iota(I32, (t, t), 1)
    causal = col <= row
    dl = dl_ref[...]
    lam = (jnp.exp(jnp.sum(dl[0:1] * dl[1:2], axis=1, keepdims=True))
           - jnp.exp(jnp.sum(dl[2:3] * dl[3:4], axis=1, keepdims=True)) + lam_init)

    def block(j, st, diag):
        j0 = pl.multiple_of(j * t, t)
        new = []
        for hh in range(nh):
            kj = k_ref[pl.ds(j0, t), _hs(hh)]
            vj = v_ref[pl.ds(j0, t), _hs(hh)]
            for mi in range(2):
                m, l, acc = st[2 * hh + mi]
                s = lax.dot_general(qs[hh][mi], kj, _NT, preferred_element_type=F32)
                if diag:
                    s = jnp.where(causal, s, NEG_BIG)
                m_new = jnp.maximum(m, jnp.max(s, axis=1, keepdims=True))
                alpha = jnp.exp2((m - m_new) * LOG2E)
                pr = jnp.exp2((s - m_new) * LOG2E)
                l = alpha * l + jnp.sum(pr, axis=1, keepdims=True)
                acc = alpha * acc + jnp.dot(pr.astype(BF16), vj, preferred_element_type=F32)
                new.append((m_new, l, acc))
        return tuple(new)

    init = tuple((jnp.full((t, 1), NEG_BIG, F32), jnp.zeros((t, 1), F32), jnp.zeros((t, HEAD_DIM), F32))
                 for _ in range(2 * nh))
    st = block(i, init, True)
    st = lax.fori_loop(0, i, lambda j, s: block(j, s, False), st)
    for hh in range(nh):
        (_, l0, a0), (_, l1, a1) = st[2 * hh], st[2 * hh + 1]
        o = a0 / l0 - lam * (a1 / l1)
        o_ref[:, _hs(hh)] = (_rms(o, g_ref[...]) * (1.0 - lam_init)).astype(o_ref.dtype)


def _moba_kernel(q_ref, k_ref, v_ref, o_ref, km_hi, km_lo, *, t, nh, nkb, grp):
    own = pl.program_id(2)
    scale = HEAD_DIM ** -0.5
    topk = min(MOBA_TOPK, nkb)

    @pl.when(own == 0)
    def _():
        km_hi[...] = jnp.zeros_like(km_hi)
        km_lo[...] = jnp.zeros_like(km_lo)
        for hh in range(nh):
            for blk in range(nkb):
                km = jnp.mean(k_ref[blk * t:(blk + 1) * t, _hs(hh)].astype(F32), axis=0, keepdims=True)
                hi = km.astype(BF16)
                km_hi[hh, blk:blk + 1, :] = hi
                km_lo[hh, blk:blk + 1, :] = (km - hi.astype(F32)).astype(BF16)

    lane = lax.broadcasted_iota(I32, (t, LANES), 1)
    lane_f = lane.astype(F32)
    row = lax.broadcasted_iota(I32, (t, t), 0)
    col = lax.broadcasted_iota(I32, (t, t), 1)
    qs, sels = [], []
    for hh in range(nh):
        q = q_ref[:, _hs(hh)]
        gate = (lax.dot_general(q, km_hi[hh], _NT, preferred_element_type=F32)
                + lax.dot_general(q, km_lo[hh], _NT, preferred_element_type=F32))
        cand = jnp.where(lane < own, gate, -jnp.inf)
        sel = jnp.zeros(gate.shape, F32)
        for _ in range(topk):
            best = jnp.max(cand, axis=1, keepdims=True)
            first = jnp.min(jnp.where(cand == best, lane_f, float(LANES)), axis=1, keepdims=True)
            hit = (lane_f == first) & (best > -jnp.inf)
            sel = jnp.where(hit, 1.0, sel)
            cand = jnp.where(hit, -jnp.inf, cand)
        qs.append(q)
        sels.append(sel)

    def attend(j, nblk, st, masks):
        j0 = pl.multiple_of(j * t, t)
        new = []
        for hh in range(nh):
            m, l, acc = st[hh]
            kj = k_ref[pl.ds(j0, nblk * t), _hs(hh)]
            vj = v_ref[pl.ds(j0, nblk * t), _hs(hh)]
            s = lax.dot_general(qs[hh], kj, _NT, preferred_element_type=F32)
            s = jnp.concatenate([jnp.where(masks[hh][b], s[:, b * t:(b + 1) * t], NEG_BIG) for b in range(nblk)], axis=1)
            m_new = jnp.maximum(m, jnp.max(s, axis=1, keepdims=True))
            alpha = jnp.exp2((m - m_new) * (scale * LOG2E))
            pr = jnp.exp2((s - m_new) * (scale * LOG2E))
            l = alpha * l + jnp.sum(pr, axis=1, keepdims=True)
            acc = alpha * acc + jnp.dot(pr.astype(BF16), vj, preferred_element_type=F32)
            new.append((m_new, l, acc))
        return tuple(new)

    init = tuple((jnp.full((t, 1), NEG_BIG, F32), jnp.zeros((t, 1), F32), jnp.zeros((t, HEAD_DIM), F32))
                 for _ in range(nh))
    st = attend(own, 1, init, [[col <= row]] * nh)

    def body(g, st):
        picked = [[jnp.max(jnp.where(lane == g * grp + b, sels[hh], 0.0), axis=1, keepdims=True) > 0.5
                   for b in range(grp)] for hh in range(nh)]
        return attend(g * grp, grp, st, picked)

    st = lax.fori_loop(0, (own + grp - 1) // grp, body, st)
    for hh in range(nh):
        _, l, acc = st[hh]
        o_ref[:, _hs(hh)] = (acc / l).astype(o_ref.dtype)


def _attention(kernel_fn, p, extra, extra_specs, scratch, *, B, S, G, c_q, c_k, c_v, t, nh, name):
    H = G // HEAD_DIM
    nq = S // t
    w = nh * HEAD_DIM
    assert H % nh == 0 and c_q % nh == 0 and c_k % nh == 0 and c_v % nh == 0
    return pl.pallas_call(
        kernel_fn,
        out_shape=jax.ShapeDtypeStruct((B * S, G), BF16),
        grid=(B, H // nh, nq),
        in_specs=[pl.BlockSpec((t, w), lambda b, h, i: (b * nq + i, c_q // nh + h)),
                  pl.BlockSpec((S, w), lambda b, h, i: (b, c_k // nh + h)),
                  pl.BlockSpec((S, w), lambda b, h, i: (b, c_v // nh + h))] + extra_specs,
        out_specs=pl.BlockSpec((t, w), lambda b, h, i: (b * nq + i, h)),
        scratch_shapes=scratch,
        compiler_params=_cparams(("arbitrary", "arbitrary", "arbitrary"), 40),
        name=name,
    )(p, p, p, *extra)


def _moe_plan(top_idx, N, tm):
    E, NK = N_EXPERTS, N * TOP_K
    nblocks = (NK + E * (tm - 1) + tm - 1) // tm
    e_flat = top_idx.reshape(NK)
    order = jnp.argsort(e_flat, stable=True).astype(I32)
    counts = jnp.sum((e_flat[:, None] == jnp.arange(E, dtype=I32)[None, :]).astype(I32), axis=0)
    starts = jnp.cumsum(counts) - counts
    nblk_e = (counts + tm - 1) // tm
    bend = jnp.cumsum(nblk_e)
    bstart = bend - nblk_e
    b = jnp.arange(nblocks, dtype=I32)
    blk_e = jnp.minimum(jnp.sum((bend[None, :] <= b[:, None]).astype(I32), axis=1), E - 1).astype(I32)
    blk_off = (b - bstart[blk_e]) * tm
    blk_nv = jnp.where(b < bend[-1], jnp.clip(counts[blk_e] - blk_off, 0, tm), 0).astype(I32)
    ri = jnp.arange(tm, dtype=I32)
    valid = ri[None, :] < blk_nv[:, None]
    src = jnp.clip(starts[blk_e][:, None] + blk_off[:, None] + ri[None, :], 0, NK - 1)
    flat = order[src]
    row_tok = jnp.where(valid, flat // TOP_K, 0).astype(I32)
    row_slot = jnp.where(valid, (flat % TOP_K) * N + flat // TOP_K, -1).astype(I32)
    return blk_e, blk_nv, row_tok, row_slot


def _rowcopy_kernel(cnt_ref, idx_hbm, src, dst, idx_smem, isem, csem, *, ch, rpt, gather):
    s = pl.program_id(0)
    ns = pl.num_programs(0)
    slot = s % 2

    def idx_copy(step, sl):
        return pltpu.make_async_copy(idx_hbm.at[step], idx_smem.at[sl], isem.at[sl])

    @pl.when(s == 0)
    def _():
        idx_copy(0, 0).start()

    idx_copy(s, slot).wait()

    @pl.when(s + 1 < ns)
    def _():
        idx_copy(s + 1, 1 - slot).start()

    def row_copy(r, t):
        blk = pl.ds(pl.multiple_of(r * rpt, rpt), rpt)
        far = pl.ds(pl.multiple_of(t * rpt, rpt), rpt)
        if gather:
            return pltpu.make_async_copy(src.at[far], dst.at[blk], csem)
        return pltpu.make_async_copy(src.at[blk], dst.at[far], csem)

    def issue(r, c):
        t = idx_smem[slot, r]
        if gather:
            row_copy(r, t).start()
        else:
            @pl.when(t >= 0)
            def _():
                row_copy(r, t).start()
        return c

    def drain_group(g, c):
        for _ in range(WAIT_GROUP):
            row_copy(0, 0).wait()
        return c

    def drain_one(r, c):
        row_copy(0, 0).wait()
        return c

    lax.fori_loop(0, ch, issue, 0, unroll=8)
    n = cnt_ref[s]
    lax.fori_loop(0, n // WAIT_GROUP, drain_group, 0)
    lax.fori_loop(0, n % WAIT_GROUP, drain_one, 0)


def _rowcopy(idx, src, n_dst_tokens, *, rpt, gather, ch=512):
    ns = idx.shape[0] // ch
    idx2 = idx.reshape(ns, ch)
    cnt = jnp.sum((idx2 >= 0).astype(I32), axis=1)
    blk = pl.BlockSpec((ch * rpt, LANES), lambda s, cnt: (s, 0))
    hbm = pl.BlockSpec(memory_space=pl.ANY)
    return pl.pallas_call(
        functools.partial(_rowcopy_kernel, ch=ch, rpt=rpt, gather=gather),
        out_shape=jax.ShapeDtypeStruct((n_dst_tokens * rpt, LANES), src.dtype),
        grid_spec=pltpu.PrefetchScalarGridSpec(
            num_scalar_prefetch=1, grid=(ns,),
            in_specs=[hbm, hbm if gather else blk],
            out_specs=blk if gather else hbm,
            scratch_shapes=[pltpu.SMEM((2, ch), I32), pltpu.SemaphoreType.DMA((2,)), pltpu.SemaphoreType.DMA]),
        compiler_params=_cparams(("arbitrary",), 32),
        name="row_gather" if gather else "row_scatter",
    )(cnt, idx2, src)


def _expert_kernel(be_ref, nv_ref, x_ref, w1_ref, b1_ref, w2_ref, b2_ref, o_ref, *, tm):
    b = pl.program_id(0)
    F = w2_ref.shape[1]
    half = w1_ref.shape[1] // 2

    @pl.when(nv_ref[b] > 0)
    def _():
        lo, hi = _unpack_halves(_load_token_rows(x_ref, tm))
        a = (jnp.dot(lo, w1_ref[0, :half, :], preferred_element_type=F32)
             + jnp.dot(hi, w1_ref[0, half:, :], preferred_element_type=F32) + b1_ref[0])
        g = jnp.minimum(a[:, :F], SWIGLU_LIMIT)
        lin = jnp.clip(a[:, F:], -SWIGLU_LIMIT, SWIGLU_LIMIT)
        act = g * jax.nn.sigmoid(SWIGLU_ALPHA * g) * (lin + 1.0)
        y = jnp.dot(act.astype(BF16), w2_ref[0], preferred_element_type=F32) + b2_ref[0]
        _store_token_rows(o_ref, _pack_halves(y))

    @pl.when(nv_ref[b] == 0)
    def _():
        o_ref[...] = jnp.zeros_like(o_ref)


def _experts(blk_e, blk_nv, xs, w1, b1, w2, b2, *, tm, rpt):
    E, D, F2 = w1.shape
    F = F2 // 2
    nblocks = xs.shape[0] // (tm * rpt)
    return pl.pallas_call(
        functools.partial(_expert_kernel, tm=tm),
        out_shape=jax.ShapeDtypeStruct(xs.shape, U32),
        grid_spec=pltpu.PrefetchScalarGridSpec(
            num_scalar_prefetch=2, grid=(nblocks,),
            in_specs=[pl.BlockSpec((tm * rpt, LANES), lambda b, be, nv: (b, 0)),
                      pl.BlockSpec((1, D, F2), lambda b, be, nv: (be[b], 0, 0)),
                      pl.BlockSpec((1, 1, F2), lambda b, be, nv: (be[b], 0, 0)),
                      pl.BlockSpec((1, F, D), lambda b, be, nv: (be[b], 0, 0)),
                      pl.BlockSpec((1, 1, D), lambda b, be, nv: (be[b], 0, 0))],
            out_specs=pl.BlockSpec((tm * rpt, LANES), lambda b, be, nv: (b, 0))),
        compiler_params=_cparams(("arbitrary",), 56),
        name="experts",
    )(blk_e, blk_nv, xs, w1, b1.reshape(E, 1, F2), w2, b2.reshape(E, 1, D))


def _moe(hpk, top_idx, w1, b1, w2, b2, *, N, rpt, tm=256):
    blk_e, blk_nv, row_tok, row_slot = _moe_plan(top_idx, N, tm)
    xs = _rowcopy(row_tok.reshape(-1), hpk, row_tok.size, rpt=rpt, gather=True)
    ys = _experts(blk_e, blk_nv, xs, w1, b1, w2, b2, tm=tm, rpt=rpt)
    return _rowcopy(row_slot.reshape(-1), ys, TOP_K * N, rpt=rpt, gather=False)


def kernel(x, c, w_ada, b_ada, ada_table, norm_pre, norm_post, w_in, w_out, conv_w, conv_b, conv_ln_g, conv_ln_b,
           diff_lambda, diff_norm_g, w_router, b_router, w1, b1, w2, b2):
    B, S, D = x.shape
    depth = w_in.shape[0]
    N = B * S
    G = D // N_GROUPS
    H = G // HEAD_DIM
    E = w_router.shape[2]
    assert E == N_EXPERTS and S % MOBA_BLOCK == 0 and G % HEAD_DIM == 0
    nkb = S // MOBA_BLOCK

    cond = _ada(c, w_ada, b_ada)
    xf = x.reshape(N, D)
    h = None
    for l in range(depth):
        lam_init = 0.8 - 0.6 * math.exp(-0.3 * l)
        tab = ada_table[l]
        if l == 0:
            (h,) = _post(xf, cond, tab, tab, S=S, mode="none", nxt="plain", g_pre=norm_pre[l, 0], i_shift=0, i_scale=1)
        p = _matmul([h], _layer_bf16(w_in, l), BF16, 1024, 512 if w_in.shape[2] % 512 == 0 else G)

        y_conv = _conv_mixer(p, conv_w[l], conv_b[l], conv_ln_g[l], conv_ln_b[l], B=B, S=S, G=G)
        t = MOBA_BLOCK
        nh = HEADS_PER_STEP
        td = math.gcd(DIFF_TILE, S)
        y_sb = _attention(functools.partial(_sb_kernel, t=t, nh=nh), p, [], [], [], B=B, S=S, G=G,
                          c_q=2 * H, c_k=3 * H, c_v=4 * H, t=t, nh=nh, name="stick_breaking")
        y_diff = _attention(
            functools.partial(_diff_kernel, t=td, nh=nh, lam_init=lam_init), p,
            [diff_lambda[l], diff_norm_g[l].reshape(1, HEAD_DIM)],
            [pl.BlockSpec((4, DIFF_QK_DIM), lambda b, h_, i: (0, 0)), pl.BlockSpec((1, HEAD_DIM), lambda b, h_, i: (0, 0))],
            [], B=B, S=S, G=G, c_q=5 * H, c_k=6 * H, c_v=7 * H, t=td, nh=nh, name="diff_attention")
        y_moba = _attention(
            functools.partial(_moba_kernel, t=t, nh=nh, nkb=nkb, grp=math.gcd(MOBA_GROUP, nkb)), p, [], [],
            [pltpu.VMEM((nh, LANES, HEAD_DIM), BF16), pltpu.VMEM((nh, LANES, HEAD_DIM), BF16)],
            B=B, S=S, G=G, c_q=8 * H, c_k=9 * H, c_v=10 * H, t=t, nh=nh, name="moba")

        y = _matmul([y_conv, y_sb, y_diff, y_moba], _layer_bf16(w_out, l), F32, 1024, 512)

        wr = jnp.zeros((D, LANES), F32).at[:, :E].set(w_router[l])
        wr_hi = wr.astype(BF16)
        wr_lo = (wr - wr_hi.astype(F32)).astype(BF16)
        br = jnp.zeros((1, LANES), F32).at[0, :E].set(b_router[l])
        xf, hpk, top_idx, gates = _post(xf, cond, tab, tab, S=S, mode="dense", nxt="router", y=y, g_post=norm_post[l, 0],
                                        g_pre=norm_pre[l, 1], router=(wr_hi, wr_lo, br), i_gate=2, i_shift=3, i_scale=4)

        rpt = D // 2 // LANES
        y4 = _moe(hpk, top_idx[:, :TOP_K], _layer_bf16(w1, l), b1[l], _layer_bf16(w2, l), b2[l], N=N, rpt=rpt)
        y4 = y4.reshape(TOP_K, N * rpt, LANES)
        if l + 1 < depth:
            xf, h = _post(xf, cond, tab, ada_table[l + 1], S=S, mode="moe", nxt="plain", y4=y4, g4=gates,
                          g_post=norm_post[l, 1], g_pre=norm_pre[l + 1, 0], i_gate=5, i_shift=0, i_scale=1)
        else:
            (xf,) = _post(xf, cond, tab, tab, S=S, mode="moe", nxt="none", y4=y4, g4=gates, g_post=norm_post[l, 1], i_gate=5)
    return xf.reshape(B, S, D)
```

```python
import functools
import math

import jax
import jax.numpy as jnp
from jax import lax
from jax.experimental import pallas as pl
from jax.experimental.pallas import tpu as pltpu

F32 = jnp.float32
BF16 = jnp.bfloat16
U32 = jnp.uint32
I32 = jnp.int32

N_GROUPS = 4
HEAD_DIM = 128
DIFF_QK_DIM = HEAD_DIM // 2
CONV_WIDTH = 31
MOBA_BLOCK = 256
MOBA_TOPK = 3
N_EXPERTS = 32
TOP_K = 4
SWIGLU_ALPHA = 1.702
SWIGLU_LIMIT = 7.0
RMS_EPS = 1e-6
LN_EPS = 1e-5
N_MOD = 6
LANES = 128
SUBLANES = 8
CONV_HALO = 32
NEG_BIG = -1e30
MIB = 1024 * 1024


def _cparams(sem, vmem_mib):
    return pltpu.CompilerParams(dimension_semantics=sem, vmem_limit_bytes=vmem_mib * MIB)


def _rms(xf, g):
    return xf * lax.rsqrt(jnp.mean(xf * xf, axis=-1, keepdims=True) + RMS_EPS) * g


def _pack_halves(a):
    k = a.shape[1] // 2
    lo = lax.bitcast_convert_type(a[:, :k].astype(BF16).astype(F32), U32)
    hi = lax.bitcast_convert_type(a[:, k:].astype(BF16).astype(F32), U32)
    return (lo >> 16) | (hi & jnp.uint32(0xFFFF0000))


def _unpack_halves(w):
    lo = lax.bitcast_convert_type(w << 16, F32).astype(BF16)
    hi = lax.bitcast_convert_type(w & jnp.uint32(0xFFFF0000), F32).astype(BF16)
    return lo, hi


def _store_token_rows(ref, words):
    m, rpt = words.shape[0], words.shape[1] // LANES
    for c in range(rpt):
        ref[pl.ds(c, m, stride=rpt), :] = words[:, c * LANES:(c + 1) * LANES]


def _load_token_rows(ref, m):
    rpt = ref.shape[0] // m
    return jnp.concatenate([ref[pl.ds(c, m, stride=rpt), :] for c in range(rpt)], axis=1)


def _ada_kernel(c_ref, w_ref, b_ref, o_ref):
    c = c_ref[...]
    a = (c * jax.nn.sigmoid(c)).astype(BF16)
    o_ref[...] = jnp.dot(a, w_ref[...].astype(BF16), preferred_element_type=F32) + b_ref[...]


def _ada(c, w_ada, b_ada):
    B, D = c.shape
    n_out = w_ada.shape[1]
    tn = 512
    rows = 8
    c8 = jnp.zeros((rows, D), F32).at[:B].set(c)
    out = pl.pallas_call(
        _ada_kernel,
        out_shape=jax.ShapeDtypeStruct((rows, n_out), F32),
        grid=(n_out // tn,),
        in_specs=[pl.BlockSpec((rows, D), lambda j: (0, 0)),
                  pl.BlockSpec((D, tn), lambda j: (0, j)),
                  pl.BlockSpec((1, tn), lambda j: (0, j))],
        out_specs=pl.BlockSpec((rows, tn), lambda j: (0, j)),
        compiler_params=_cparams(("arbitrary",), 40),
        name="ada",
    )(c8, w_ada, b_ada.reshape(1, n_out))
    return out[:B].reshape(B, N_MOD, D)


def _router_topk(hf, wr_hi_ref, wr_lo_ref, br_ref):
    h_hi = hf.astype(BF16)
    h_lo = (hf - h_hi.astype(F32)).astype(BF16)
    logits = (jnp.dot(h_hi, wr_hi_ref[...], preferred_element_type=F32)
              + jnp.dot(h_lo, wr_hi_ref[...], preferred_element_type=F32)
              + jnp.dot(h_hi, wr_lo_ref[...], preferred_element_type=F32)) + br_ref[...]
    lane = lax.broadcasted_iota(I32, logits.shape, 1)
    lane_f = lane.astype(F32)
    l = jnp.where(lane < N_EXPERTS, logits, -jnp.inf)
    idx_out = jnp.zeros(logits.shape, F32)
    val_out = jnp.full(logits.shape, -jnp.inf, F32)
    m0 = None
    for k in range(TOP_K):
        m = jnp.max(l, axis=1, keepdims=True)
        am = jnp.min(jnp.where(l == m, lane_f, float(LANES)), axis=1, keepdims=True)
        idx_out = jnp.where(lane == k, am, idx_out)
        val_out = jnp.where(lane == k, m, val_out)
        l = jnp.where(lane_f == am, -jnp.inf, l)
        if k == 0:
            m0 = m
    e = jnp.where(lane < TOP_K, jnp.exp(val_out - m0), 0.0)
    gates = e / jnp.sum(e, axis=1, keepdims=True)
    return idx_out.astype(I32), gates


def _post_kernel(*refs, mode, nxt, i_gate, i_shift, i_scale):
    it = iter(refs)
    x_ref = next(it)
    if mode == "dense":
        y_ref = next(it)
    elif mode == "moe":
        y4_ref = next(it)
        g4_ref = next(it)
    cond_ref = next(it)
    tab_ref = next(it)
    tabn_ref = next(it)
    gpost_ref = next(it) if mode != "none" else None
    gpre_ref = next(it) if nxt != "none" else None
    if nxt == "router":
        wr_hi_ref, wr_lo_ref, br_ref = next(it), next(it), next(it)
    outs = list(it)

    mod = cond_ref[0] + tab_ref[...]
    modn = cond_ref[0] + tabn_ref[...]
    x = x_ref[...]
    if mode == "dense":
        y = y_ref[...].astype(F32)
    elif mode == "moe":
        g4 = g4_ref[...]
        y = None
        for k in range(TOP_K):
            lo, hi = _unpack_halves(_load_token_rows(y4_ref.at[k], x.shape[0]))
            yk = jnp.concatenate([lo.astype(F32), hi.astype(F32)], axis=1) * g4[:, k:k + 1]
            y = yk if y is None else y + yk
    o = 0
    if mode != "none":
        x = x + mod[i_gate:i_gate + 1, :] * _rms(y, gpost_ref[...])
        outs[o][...] = x
        o += 1
    if nxt != "none":
        hf = _rms(x, gpre_ref[...]) * (1.0 + modn[i_scale:i_scale + 1, :]) + modn[i_shift:i_shift + 1, :]
        if nxt == "plain":
            outs[o][...] = hf.astype(BF16)
        else:
            _store_token_rows(outs[o], _pack_halves(hf))
            idx, gates = _router_topk(hf, wr_hi_ref, wr_lo_ref, br_ref)
            outs[o + 1][...] = idx
            outs[o + 2][...] = gates


def _post(x, cond, tab, tab_next, *, S, mode, nxt, y=None, y4=None, g4=None, g_post=None, g_pre=None,
          router=None, i_gate=0, i_shift=0, i_scale=0, tm=256):
    N, D = x.shape
    nb = S // tm
    rpt = D // 2 // LANES
    row = lambda i: (i, 0)
    args, specs = [x], [pl.BlockSpec((tm, D), row)]
    if mode == "dense":
        args.append(y)
        specs.append(pl.BlockSpec((tm, D), row))
    elif mode == "moe":
        args += [y4, g4]
        specs += [pl.BlockSpec((TOP_K, tm * rpt, LANES), lambda i: (0, i, 0)), pl.BlockSpec((tm, LANES), row)]
    args += [cond, tab, tab_next]
    specs += [pl.BlockSpec((1, N_MOD, D), lambda i: (i // nb, 0, 0)), pl.BlockSpec((N_MOD, D), lambda i: (0, 0)),
              pl.BlockSpec((N_MOD, D), lambda i: (0, 0))]
    if mode != "none":
        args.append(g_post.reshape(1, D))
        specs.append(pl.BlockSpec((1, D), lambda i: (0, 0)))
    if nxt != "none":
        args.append(g_pre.reshape(1, D))
        specs.append(pl.BlockSpec((1, D), lambda i: (0, 0)))
    if nxt == "router":
        args += list(router)
        specs += [pl.BlockSpec((D, LANES), lambda i: (0, 0)), pl.BlockSpec((D, LANES), lambda i: (0, 0)),
                  pl.BlockSpec((1, LANES), lambda i: (0, 0))]
    out_shape, out_specs = [], []
    if mode != "none":
        out_shape.append(jax.ShapeDtypeStruct((N, D), F32))
        out_specs.append(pl.BlockSpec((tm, D), row))
    if nxt == "plain":
        out_shape.append(jax.ShapeDtypeStruct((N, D), BF16))
        out_specs.append(pl.BlockSpec((tm, D), row))
    elif nxt == "router":
        out_shape += [jax.ShapeDtypeStruct((N * rpt, LANES), U32), jax.ShapeDtypeStruct((N, LANES), I32),
                      jax.ShapeDtypeStruct((N, LANES), F32)]
        out_specs += [pl.BlockSpec((tm * rpt, LANES), row), pl.BlockSpec((tm, LANES), row), pl.BlockSpec((tm, LANES), row)]
    return pl.pallas_call(
        functools.partial(_post_kernel, mode=mode, nxt=nxt, i_gate=i_gate, i_shift=i_shift, i_scale=i_scale),
        out_shape=out_shape, grid=(N // tm,), in_specs=specs, out_specs=out_specs,
        compiler_params=_cparams(("arbitrary",), 56),
        name=f"post_{mode}_{nxt}",
    )(*args)


def _mm_kernel(*refs, n_a):
    a_refs, w_ref, o_ref = refs[:n_a], refs[n_a], refs[n_a + 1]
    acc, off = None, 0
    for a in a_refs:
        kg = a.shape[1]
        part = jnp.dot(a[...], w_ref[off:off + kg, :], preferred_element_type=F32)
        acc = part if acc is None else acc + part
        off += kg
    o_ref[...] = acc.astype(o_ref.dtype)


def _matmul(a_list, w, out_dtype, tm, tn):
    N = a_list[0].shape[0]
    K, n_out = w.shape
    specs = [pl.BlockSpec((tm, a.shape[1]), lambda i, j: (i, 0)) for a in a_list]
    specs.append(pl.BlockSpec((K, tn), lambda i, j: (0, j)))
    return pl.pallas_call(
        functools.partial(_mm_kernel, n_a=len(a_list)),
        out_shape=jax.ShapeDtypeStruct((N, n_out), out_dtype),
        grid=(N // tm, n_out // tn), in_specs=specs,
        out_specs=pl.BlockSpec((tm, tn), lambda i, j: (i, j)),
        compiler_params=_cparams(("arbitrary", "arbitrary"), 56),
        name="matmul",
    )(*a_list, w)


def _cast_kernel(x_ref, o_ref):
    o_ref[...] = x_ref[0].astype(o_ref.dtype)


def _layer_bf16(w, l):
    C = w.shape[-1]
    R = math.prod(w.shape[1:-1])
    tr = math.gcd(R, max(8, pl.next_power_of_2(CAST_BLOCK_BYTES // (4 * C) + 1) // 2))
    out = pl.pallas_call(
        _cast_kernel,
        out_shape=jax.ShapeDtypeStruct((R, C), BF16),
        grid=(R // tr,),
        in_specs=[pl.BlockSpec((1, tr, C), lambda i: (l, i, 0))],
        out_specs=pl.BlockSpec((tr, C), lambda i: (i, 0)),
        compiler_params=_cparams(("arbitrary",), 40),
        name="cast_bf16",
    )(w.reshape(w.shape[0], R, C))
    return out.reshape(w.shape[1:])


def _conv_kernel(val_ref, gate_ref, pval_ref, pgate_ref, w_ref, b_ref, lg_ref, lb_ref, o_ref, ubuf, ybuf, shbuf, *, ts):
    G = o_ref.shape[1]
    i = pl.program_id(1)
    v = val_ref[...].astype(F32)
    g = gate_ref[...].astype(F32)
    ubuf[CONV_HALO:CONV_HALO + ts, :] = v * jax.nn.sigmoid(g)
    pv = pval_ref[...].astype(F32)
    pg = pgate_ref[...].astype(F32)
    ubuf[0:CONV_HALO, :] = jnp.where(i > 0, pv * jax.nn.sigmoid(pg), 0.0)
    base = CONV_HALO - (CONV_WIDTH - 1)
    rs = 128
    span = ts + CONV_HALO - SUBLANES

    def chan(c, carry):
        c0 = pl.multiple_of(c * LANES, LANES)
        win = ubuf[:, pl.ds(c0, LANES)]
        for s in range(1, SUBLANES):
            shbuf[s - 1, 0:span, :] = win[s:s + span]
        for r0 in range(0, ts, rs):
            acc = jnp.zeros((rs, LANES), F32)
            for j in range(CONV_WIDTH):
                q, s = divmod(base + j, SUBLANES)
                a0 = r0 + SUBLANES * q
                src = ubuf[a0:a0 + rs, pl.ds(c0, LANES)] if s == 0 else shbuf[s - 1, a0:a0 + rs, :]
                acc = acc + w_ref[j:j + 1, pl.ds(c0, LANES)] * src
            ybuf[r0:r0 + rs, pl.ds(c0, LANES)] = acc
        return carry

    lax.fori_loop(0, G // LANES, chan, 0)
    for r0 in range(0, ts, 64):
        y = ybuf[r0:r0 + 64, :] + b_ref[...]
        mu = jnp.mean(y, axis=-1, keepdims=True)
        d = y - mu
        var = jnp.mean(d * d, axis=-1, keepdims=True)
        z = d * lax.rsqrt(var + LN_EPS) * lg_ref[...] + lb_ref[...]
        o_ref[r0:r0 + 64, :] = (z * jax.nn.sigmoid(z)).astype(o_ref.dtype)


def _conv_mixer(p, conv_w, conv_b, ln_g, ln_b, *, B, S, G, ts=256):
    nb = S // ts
    hb = ts // CONV_HALO
    wpad = jnp.zeros((CONV_HALO, G), F32).at[:CONV_WIDTH].set(conv_w)
    cur = lambda col: pl.BlockSpec((ts, G), lambda b, i: (b * nb + i, col))
    prev = lambda col: pl.BlockSpec((CONV_HALO, G), lambda b, i: (jnp.maximum((b * nb + i) * hb - 1, 0), col))
    vec = pl.BlockSpec((1, G), lambda b, i: (0, 0))
    return pl.pallas_call(
        functools.partial(_conv_kernel, ts=ts),
        out_shape=jax.ShapeDtypeStruct((B * S, G), BF16),
        grid=(B, nb),
        in_specs=[cur(0), cur(1), prev(0), prev(1), pl.BlockSpec((CONV_HALO, G), lambda b, i: (0, 0)), vec, vec, vec],
        out_specs=pl.BlockSpec((ts, G), lambda b, i: (b * nb + i, 0)),
        scratch_shapes=[pltpu.VMEM((ts + CONV_HALO, G), F32), pltpu.VMEM((ts, G), F32),
                        pltpu.VMEM((SUBLANES - 1, ts + CONV_HALO, LANES), F32)],
        compiler_params=_cparams(("arbitrary", "arbitrary"), 32),
        name="conv_mixer",
    )(p, p, p, p, wpad, conv_b.reshape(1, G), ln_g.reshape(1, G), ln_b.reshape(1, G))


_NT = (((1,), (1,)), ((), ()))
HEADS_PER_STEP = 2
SB_UNDERFLOW = 110.0
DIFF_TILE = 512
MOBA_GROUP = 4
CAST_BLOCK_BYTES = 8 * MIB
ROWCOPY_CHUNK = 2048
ISSUE_UNROLL = 8
WAIT_GROUP = 64
LOG2E = 1.4426950408889634


def _hs(hh):
    return slice(hh * HEAD_DIM, (hh + 1) * HEAD_DIM)


def _sb_kernel(q_ref, k_ref, v_ref, o_ref, *, t, nh):
    i = pl.program_id(2)
    qs = [q_ref[:, _hs(hh)] for hh in range(nh)]
    scale = HEAD_DIM ** -0.5
    row = lax.broadcasted_iota(I32, (t, t), 0)
    col = lax.broadcasted_iota(I32, (t, t), 1)
    tri = jnp.where(row >= col, 1.0, 0.0).astype(BF16)
    past = col < row

    def block(j, st, diag):
        j0 = pl.multiple_of(j * t, t)
        new = []
        for hh in range(nh):
            carry, acc = st[hh]
            kj = k_ref[pl.ds(j0, t), _hs(hh)]
            vj = v_ref[pl.ds(j0, t), _hs(hh)]
            z = lax.dot_general(qs[hh], kj, _NT, preferred_element_type=F32) * scale
            sp = jnp.maximum(z, 0.0) + jnp.log(1.0 + jnp.exp(-jnp.abs(z)))
            u = jnp.where(past, sp, 0.0) if diag else sp
            u_hi = u.astype(BF16)
            u_lo = (u - u_hi.astype(F32)).astype(BF16)
            tl = jnp.dot(u_hi, tri, preferred_element_type=F32) + jnp.dot(u_lo, tri, preferred_element_type=F32)
            e = z - (tl + carry)
            if diag:
                e = jnp.where(past, e, -jnp.inf)
            a = jnp.exp(e)
            acc = acc + jnp.dot(a.astype(BF16), vj, preferred_element_type=F32)
            new.append((carry + tl[:, 0:1], acc))
        return tuple(new)

    def live(st):
        lo = st[0][0]
        for hh in range(1, nh):
            lo = jnp.minimum(lo, st[hh][0])
        return (jnp.min(lo) < SB_UNDERFLOW).astype(I32)

    def body(c):
        st = block(i - 1 - c[0], c[2], False)
        return c[0] + 1, live(st), st

    init = tuple((jnp.zeros((t, 1), F32), jnp.zeros((t, HEAD_DIM), F32)) for _ in range(nh))
    st = block(i, init, True)
    _, _, st = lax.while_loop(lambda c: (c[0] < i) & (c[1] > 0), body, (jnp.int32(0), live(st), st))
    for hh in range(nh):
        o_ref[:, _hs(hh)] = st[hh][1].astype(o_ref.dtype)


def _diff_kernel(q_ref, k_ref, v_ref, dl_ref, g_ref, o_ref, *, t, nh, lam_init):
    i = pl.program_id(2)
    scale = DIFF_QK_DIM ** -0.5
    assert math.frexp(scale)[0] == 0.5
    lane = lax.broadcasted_iota(I32, (t, HEAD_DIM), 1)
    qs = []
    for hh in range(nh):
        q = q_ref[:, _hs(hh)] * scale
        zero = jnp.zeros_like(q)
        qs.append((jnp.where(lane < DIFF_QK_DIM, q, zero), jnp.where(lane >= DIFF_QK_DIM, q, zero)))
    row = lax.broadcasted_iota(I32, (t, t), 0)
    col = lax.broadcasted_iota(I32, (t, t), 1)
    causal = col <= row
    dl = dl_ref[...]
    lam = (jnp.exp(jnp.sum(dl[0:1] * dl[1:2], axis=1, keepdims=True))
           - jnp.exp(jnp.sum(dl[2:3] * dl[3:4], axis=1, keepdims=True)) + lam_init)

    def block(j, st, diag):
        j0 = pl.multiple_of(j * t, t)
        new = []
        for hh in range(nh):
            kj = k_ref[pl.ds(j0, t), _hs(hh)]
            vj = v_ref[pl.ds(j0, t), _hs(hh)]
            for mi in range(2):
                m, l, acc = st[2 * hh + mi]
                s = lax.dot_general(qs[hh][mi], kj, _NT, preferred_element_type=F32)
                if diag:
                    s = jnp.where(causal, s, NEG_BIG)
                m_new = jnp.maximum(m, jnp.max(s, axis=1, keepdims=True))
                alpha = jnp.exp2((m - m_new) * LOG2E)
                pr = jnp.exp2((s - m_new) * LOG2E)
                l = alpha * l + jnp.sum(pr, axis=1, keepdims=True)
                acc = alpha * acc + jnp.dot(pr.astype(BF16), vj, preferred_element_type=F32)
                new.append((m_new, l, acc))
        return tuple(new)

    init = tuple((jnp.full((t, 1), NEG_BIG, F32), jnp.zeros((t, 1), F32), jnp.zeros((t, HEAD_DIM), F32))
                 for _ in range(2 * nh))
    st = block(i, init, True)
    st = lax.fori_loop(0, i, lambda j, s: block(j, s, False), st)
    for hh in range(nh):
        (_, l0, a0), (_, l1, a1) = st[2 * hh], st[2 * hh + 1]
        o = a0 / l0 - lam * (a1 / l1)
        o_ref[:, _hs(hh)] = (_rms(o, g_ref[...]) * (1.0 - lam_init)).astype(o_ref.dtype)


def _moba_kernel(q_ref, k_ref, v_ref, o_ref, km_hi, km_lo, *, t, nh, nkb, grp):
    own = pl.program_id(2)
    scale = HEAD_DIM ** -0.5
    topk = min(MOBA_TOPK, nkb)

    @pl.when(own == 0)
    def _():
        km_hi[...] = jnp.zeros_like(km_hi)
        km_lo[...] = jnp.zeros_like(km_lo)
        for hh in range(nh):
            for blk in range(nkb):
                km = jnp.mean(k_ref[blk * t:(blk + 1) * t, _hs(hh)].astype(F32), axis=0, keepdims=True)
                hi = km.astype(BF16)
                km_hi[hh, blk:blk + 1, :] = hi
                km_lo[hh, blk:blk + 1, :] = (km - hi.astype(F32)).astype(BF16)

    lane = lax.broadcasted_iota(I32, (t, LANES), 1)
    lane_f = lane.astype(F32)
    row = lax.broadcasted_iota(I32, (t, t), 0)
    col = lax.broadcasted_iota(I32, (t, t), 1)
    qs, sels = [], []
    for hh in range(nh):
        q = q_ref[:, _hs(hh)]
        gate = (lax.dot_general(q, km_hi[hh], _NT, preferred_element_type=F32)
                + lax.dot_general(q, km_lo[hh], _NT, preferred_element_type=F32))
        cand = jnp.where(lane < own, gate, -jnp.inf)
        sel = jnp.zeros(gate.shape, F32)
        for _ in range(topk):
            best = jnp.max(cand, axis=1, keepdims=True)
            first = jnp.min(jnp.where(cand == best, lane_f, float(LANES)), axis=1, keepdims=True)
            hit = (lane_f == first) & (best > -jnp.inf)
            sel = jnp.where(hit, 1.0, sel)
            cand = jnp.where(hit, -jnp.inf, cand)
        qs.append(q)
        sels.append(sel)

    def attend(j, nblk, st, masks):
        j0 = pl.multiple_of(j * t, t)
        new = []
        for hh in range(nh):
            m, l, acc = st[hh]
            kj = k_ref[pl.ds(j0, nblk * t), _hs(hh)]
            vj = v_ref[pl.ds(j0, nblk * t), _hs(hh)]
            s = lax.dot_general(qs[hh], kj, _NT, preferred_element_type=F32)
            s = jnp.concatenate([jnp.where(masks[hh][b], s[:, b * t:(b + 1) * t], NEG_BIG) for b in range(nblk)], axis=1)
            m_new = jnp.maximum(m, jnp.max(s, axis=1, keepdims=True))
            alpha = jnp.exp2((m - m_new) * (scale * LOG2E))
            pr = jnp.exp2((s - m_new) * (scale * LOG2E))
            l = alpha * l + jnp.sum(pr, axis=1, keepdims=True)
            acc = alpha * acc + jnp.dot(pr.astype(BF16), vj, preferred_element_type=F32)
            new.append((m_new, l, acc))
        return tuple(new)

    init = tuple((jnp.full((t, 1), NEG_BIG, F32), jnp.zeros((t, 1), F32), jnp.zeros((t, HEAD_DIM), F32))
                 for _ in range(nh))
    st = attend(own, 1, init, [[col <= row]] * nh)

    def body(g, st):
        picked = [[jnp.max(jnp.where(lane == g * grp + b, sels[hh], 0.0), axis=1, keepdims=True) > 0.5
                   for b in range(grp)] for hh in range(nh)]
        return attend(g * grp, grp, st, picked)

    st = lax.fori_loop(0, (own + grp - 1) // grp, body, st)
    for hh in range(nh):
        _, l, acc = st[hh]
        o_ref[:, _hs(hh)] = (acc / l).astype(o_ref.dtype)


def _attention(kernel_fn, p, extra, extra_specs, scratch, *, B, S, G, c_q, c_k, c_v, t, nh, name):
    H = G // HEAD_DIM
    nq = S // t
    w = nh * HEAD_DIM
    assert H % nh == 0 and c_q % nh == 0 and c_k % nh == 0 and c_v % nh == 0
    return pl.pallas_call(
        kernel_fn,
        out_shape=jax.ShapeDtypeStruct((B * S, G), BF16),
        grid=(B, H // nh, nq),
        in_specs=[pl.BlockSpec((t, w), lambda b, h, i: (b * nq + i, c_q // nh + h)),
                  pl.BlockSpec((S, w), lambda b, h, i: (b, c_k // nh + h)),
                  pl.BlockSpec((S, w), lambda b, h, i: (b, c_v // nh + h))] + extra_specs,
        out_specs=pl.BlockSpec((t, w), lambda b, h, i: (b * nq + i, h)),
        scratch_shapes=scratch,
        compiler_params=_cparams(("arbitrary", "arbitrary", "arbitrary"), 40),
        name=name,
    )(p, p, p, *extra)


def _moe_plan(top_idx, N, tm):
    E, NK = N_EXPERTS, N * TOP_K
    nblocks = (NK + E * (tm - 1) + tm - 1) // tm
    e_flat = top_idx.reshape(NK)
    order = jnp.argsort(e_flat, stable=True).astype(I32)
    counts = jnp.sum((e_flat[:, None] == jnp.arange(E, dtype=I32)[None, :]).astype(I32), axis=0)
    starts = jnp.cumsum(counts) - counts
    nblk_e = (counts + tm - 1) // tm
    bend = jnp.cumsum(nblk_e)
    bstart = bend - nblk_e
    b = jnp.arange(nblocks, dtype=I32)
    blk_e = jnp.minimum(jnp.sum((bend[None, :] <= b[:, None]).astype(I32), axis=1), E - 1).astype(I32)
    blk_off = (b - bstart[blk_e]) * tm
    blk_nv = jnp.where(b < bend[-1], jnp.clip(counts[blk_e] - blk_off, 0, tm), 0).astype(I32)
    ri = jnp.arange(tm, dtype=I32)
    valid = ri[None, :] < blk_nv[:, None]
    src = jnp.clip(starts[blk_e][:, None] + blk_off[:, None] + ri[None, :], 0, NK - 1)
    flat = order[src]
    row_tok = jnp.where(valid, flat // TOP_K, 0).astype(I32)
    row_slot = jnp.where(valid, (flat % TOP_K) * N + flat // TOP_K, -1).astype(I32)
    return blk_e, blk_nv, row_tok, row_slot


def _rowcopy_kernel(cnt_ref, idx_hbm, src, dst, idx_smem, isem, csem, *, ch, rpt, gather):
    s = pl.program_id(0)
    ns = pl.num_programs(0)
    slot = s % 2

    def idx_copy(step, sl):
        return pltpu.make_async_copy(idx_hbm.at[step], idx_smem.at[sl], isem.at[sl])

    @pl.when(s == 0)
    def _():
        idx_copy(0, 0).start()

    idx_copy(s, slot).wait()

    @pl.when(s + 1 < ns)
    def _():
        idx_copy(s + 1, 1 - slot).start()

    def row_copy(r, t):
        blk = pl.ds(pl.multiple_of(r * rpt, rpt), rpt)
        far = pl.ds(pl.multiple_of(t * rpt, rpt), rpt)
        if gather:
            return pltpu.make_async_copy(src.at[far], dst.at[blk], csem)
        return pltpu.make_async_copy(src.at[blk], dst.at[far], csem)

    def issue(g, c):
        for u in range(ISSUE_UNROLL):
            r = g * ISSUE_UNROLL + u
            t = idx_smem[slot, r]
            if gather:
                row_copy(r, t).start(priority=u % 2)
            else:
                @pl.when(t >= 0)
                def _():
                    row_copy(r, t).start(priority=u % 2)
        return c

    def drain_group(g, c):
        for _ in range(WAIT_GROUP):
            row_copy(0, 0).wait()
        return c

    def drain_one(r, c):
        row_copy(0, 0).wait()
        return c

    lax.fori_loop(0, ch // ISSUE_UNROLL, issue, 0)
    n = cnt_ref[s]
    lax.fori_loop(0, n // WAIT_GROUP, drain_group, 0)
    lax.fori_loop(0, n % WAIT_GROUP, drain_one, 0)


def _rowcopy(idx, src, n_dst_tokens, *, rpt, gather, ch=ROWCOPY_CHUNK):
    ch = math.gcd(ch, idx.shape[0])
    ns = idx.shape[0] // ch
    idx2 = idx.reshape(ns, ch)
    cnt = jnp.sum((idx2 >= 0).astype(I32), axis=1)
    blk = pl.BlockSpec((ch * rpt, LANES), lambda s, cnt: (s, 0))
    hbm = pl.BlockSpec(memory_space=pl.ANY)
    return pl.pallas_call(
        functools.partial(_rowcopy_kernel, ch=ch, rpt=rpt, gather=gather),
        out_shape=jax.ShapeDtypeStruct((n_dst_tokens * rpt, LANES), src.dtype),
        grid_spec=pltpu.PrefetchScalarGridSpec(
            num_scalar_prefetch=1, grid=(ns,),
            in_specs=[hbm, hbm if gather else blk],
            out_specs=blk if gather else hbm,
            scratch_shapes=[pltpu.SMEM((2, ch), I32), pltpu.SemaphoreType.DMA((2,)), pltpu.SemaphoreType.DMA]),
        compiler_params=_cparams(("arbitrary",), 48),
        name="row_gather" if gather else "row_scatter",
    )(cnt, idx2, src)


def _expert_kernel(be_ref, nv_ref, x_ref, w1_ref, b1_ref, w2_ref, b2_ref, o_ref, *, tm):
    b = pl.program_id(0)
    F = w2_ref.shape[1]
    half = w1_ref.shape[1] // 2

    @pl.when(nv_ref[b] > 0)
    def _():
        lo, hi = _unpack_halves(_load_token_rows(x_ref, tm))
        a = (jnp.dot(lo, w1_ref[0, :half, :], preferred_element_type=F32)
             + jnp.dot(hi, w1_ref[0, half:, :], preferred_element_type=F32) + b1_ref[0])
        g = jnp.minimum(a[:, :F], SWIGLU_LIMIT)
        lin = jnp.clip(a[:, F:], -SWIGLU_LIMIT, SWIGLU_LIMIT)
        act = g * jax.nn.sigmoid(SWIGLU_ALPHA * g) * (lin + 1.0)
        y = jnp.dot(act.astype(BF16), w2_ref[0], preferred_element_type=F32) + b2_ref[0]
        _store_token_rows(o_ref, _pack_halves(y))

    @pl.when(nv_ref[b] == 0)
    def _():
        o_ref[...] = jnp.zeros_like(o_ref)


def _experts(blk_e, blk_nv, xs, w1, b1, w2, b2, *, tm, rpt):
    E, D, F2 = w1.shape
    F = F2 // 2
    nblocks = xs.shape[0] // (tm * rpt)
    return pl.pallas_call(
        functools.partial(_expert_kernel, tm=tm),
        out_shape=jax.ShapeDtypeStruct(xs.shape, U32),
        grid_spec=pltpu.PrefetchScalarGridSpec(
            num_scalar_prefetch=2, grid=(nblocks,),
            in_specs=[pl.BlockSpec((tm * rpt, LANES), lambda b, be, nv: (b, 0)),
                      pl.BlockSpec((1, D, F2), lambda b, be, nv: (be[b], 0, 0)),
                      pl.BlockSpec((1, 1, F2), lambda b, be, nv: (be[b], 0, 0)),
                      pl.BlockSpec((1, F, D), lambda b, be, nv: (be[b], 0, 0)),
                      pl.BlockSpec((1, 1, D), lambda b, be, nv: (be[b], 0, 0))],
            out_specs=pl.BlockSpec((tm * rpt, LANES), lambda b, be, nv: (b, 0))),
        compiler_params=_cparams(("arbitrary",), 56),
        name="experts",
    )(blk_e, blk_nv, xs, w1, b1.reshape(E, 1, F2), w2, b2.reshape(E, 1, D))


def _moe(hpk, top_idx, w1, b1, w2, b2, *, N, rpt, tm=256):
    blk_e, blk_nv, row_tok, row_slot = _moe_plan(top_idx, N, tm)
    xs = _rowcopy(row_tok.reshape(-1), hpk, row_tok.size, rpt=rpt, gather=True)
    ys = _experts(blk_e, blk_nv, xs, w1, b1, w2, b2, tm=tm, rpt=rpt)
    return _rowcopy(row_slot.reshape(-1), ys, TOP_K * N, rpt=rpt, gather=False)


def kernel(x, c, w_ada, b_ada, ada_table, norm_pre, norm_post, w_in, w_out, conv_w, conv_b, conv_ln_g, conv_ln_b,
           diff_lambda, diff_norm_g, w_router, b_router, w1, b1, w2, b2):
    B, S, D = x.shape
    depth = w_in.shape[0]
    N = B * S
    G = D // N_GROUPS
    H = G // HEAD_DIM
    E = w_router.shape[2]
    assert E == N_EXPERTS and S % MOBA_BLOCK == 0 and G % HEAD_DIM == 0
    nkb = S // MOBA_BLOCK

    cond = _ada(c, w_ada, b_ada)
    xf = x.reshape(N, D)
    h = None
    for l in range(depth):
        lam_init = 0.8 - 0.6 * math.exp(-0.3 * l)
        tab = ada_table[l]
        if l == 0:
            (h,) = _post(xf, cond, tab, tab, S=S, mode="none", nxt="plain", g_pre=norm_pre[l, 0], i_shift=0, i_scale=1)
        p = _matmul([h], _layer_bf16(w_in, l), BF16, 1024, 512 if w_in.shape[2] % 512 == 0 else G)

        y_conv = _conv_mixer(p, conv_w[l], conv_b[l], conv_ln_g[l], conv_ln_b[l], B=B, S=S, G=G)
        t = MOBA_BLOCK
        nh = HEADS_PER_STEP
        td = math.gcd(DIFF_TILE, S)
        y_sb = _attention(functools.partial(_sb_kernel, t=t, nh=nh), p, [], [], [], B=B, S=S, G=G,
                          c_q=2 * H, c_k=3 * H, c_v=4 * H, t=t, nh=nh, name="stick_breaking")
        y_diff = _attention(
            functools.partial(_diff_kernel, t=td, nh=nh, lam_init=lam_init), p,
            [diff_lambda[l], diff_norm_g[l].reshape(1, HEAD_DIM)],
            [pl.BlockSpec((4, DIFF_QK_DIM), lambda b, h_, i: (0, 0)), pl.BlockSpec((1, HEAD_DIM), lambda b, h_, i: (0, 0))],
            [], B=B, S=S, G=G, c_q=5 * H, c_k=6 * H, c_v=7 * H, t=td, nh=nh, name="diff_attention")
        y_moba = _attention(
            functools.partial(_moba_kernel, t=t, nh=nh, nkb=nkb, grp=math.gcd(MOBA_GROUP, nkb)), p, [], [],
            [pltpu.VMEM((nh, LANES, HEAD_DIM), BF16), pltpu.VMEM((nh, LANES, HEAD_DIM), BF16)],
            B=B, S=S, G=G, c_q=8 * H, c_k=9 * H, c_v=10 * H, t=t, nh=nh, name="moba")

        y = _matmul([y_conv, y_sb, y_diff, y_moba], _layer_bf16(w_out, l), BF16, 1024, 512)

        wr = jnp.zeros((D, LANES), F32).at[:, :E].set(w_router[l])
        wr_hi = wr.astype(BF16)
        wr_lo = (wr - wr_hi.astype(F32)).astype(BF16)
        br = jnp.zeros((1, LANES), F32).at[0, :E].set(b_router[l])
        xf, hpk, top_idx, gates = _post(xf, cond, tab, tab, S=S, mode="dense", nxt="router", y=y, g_post=norm_post[l, 0],
                                        g_pre=norm_pre[l, 1], router=(wr_hi, wr_lo, br), i_gate=2, i_shift=3, i_scale=4)

        rpt = D // 2 // LANES
        y4 = _moe(hpk, top_idx[:, :TOP_K], _layer_bf16(w1, l), b1[l], _layer_bf16(w2, l), b2[l], N=N, rpt=rpt)
        y4 = y4.reshape(TOP_K, N * rpt, LANES)
        if l + 1 < depth:
            xf, h = _post(xf, cond, tab, ada_table[l + 1], S=S, mode="moe", nxt="plain", y4=y4, g4=gates,
                          g_post=norm_post[l, 1], g_pre=norm_pre[l + 1, 0], i_gate=5, i_shift=0, i_scale=1)
        else:
            (xf,) = _post(xf, cond, tab, tab, S=S, mode="moe", nxt="none", y4=y4, g4=gates, g_post=norm_post[l, 1], i_gate=5)
    return xf.reshape(B, S, D)
```

```python
import functools
import math

import jax
import jax.numpy as jnp
from jax import lax
from jax.experimental import pallas as pl
from jax.experimental.pallas import tpu as pltpu

F32 = jnp.float32
BF16 = jnp.bfloat16
U32 = jnp.uint32
I32 = jnp.int32

N_GROUPS = 4
HEAD_DIM = 128
DIFF_QK_DIM = HEAD_DIM // 2
CONV_WIDTH = 31
MOBA_BLOCK = 256
MOBA_TOPK = 3
N_EXPERTS = 32
TOP_K = 4
SWIGLU_ALPHA = 1.702
SWIGLU_LIMIT = 7.0
RMS_EPS = 1e-6
LN_EPS = 1e-5
N_MOD = 6
LANES = 128
SUBLANES = 8
CONV_HALO = 32
NEG_BIG = -1e30
MIB = 1024 * 1024


def _cparams(sem, vmem_mib):
    return pltpu.CompilerParams(dimension_semantics=sem, vmem_limit_bytes=vmem_mib * MIB)


def _rms(xf, g):
    return xf * lax.rsqrt(jnp.mean(xf * xf, axis=-1, keepdims=True) + RMS_EPS) * g


def _pack_halves(a):
    k = a.shape[1] // 2
    lo = lax.bitcast_convert_type(a[:, :k].astype(BF16).astype(F32), U32)
    hi = lax.bitcast_convert_type(a[:, k:].astype(BF16).astype(F32), U32)
    return (lo >> 16) | (hi & jnp.uint32(0xFFFF0000))


def _unpack_halves(w):
    lo = lax.bitcast_convert_type(w << 16, F32).astype(BF16)
    hi = lax.bitcast_convert_type(w & jnp.uint32(0xFFFF0000), F32).astype(BF16)
    return lo, hi


def _store_token_rows(ref, words):
    m, rpt = words.shape[0], words.shape[1] // LANES
    for c in range(rpt):
        ref[pl.ds(c, m, stride=rpt), :] = words[:, c * LANES:(c + 1) * LANES]


def _load_token_rows(ref, m):
    rpt = ref.shape[0] // m
    return jnp.concatenate([ref[pl.ds(c, m, stride=rpt), :] for c in range(rpt)], axis=1)


def _ada_kernel(c_ref, w_ref, b_ref, o_ref):
    c = c_ref[...]
    a = (c * jax.nn.sigmoid(c)).astype(BF16)
    o_ref[...] = jnp.dot(a, w_ref[...].astype(BF16), preferred_element_type=F32) + b_ref[...]


def _ada(c, w_ada, b_ada):
    B, D = c.shape
    n_out = w_ada.shape[1]
    tn = 512
    rows = 8
    c8 = jnp.zeros((rows, D), F32).at[:B].set(c)
    out = pl.pallas_call(
        _ada_kernel,
        out_shape=jax.ShapeDtypeStruct((rows, n_out), F32),
        grid=(n_out // tn,),
        in_specs=[pl.BlockSpec((rows, D), lambda j: (0, 0)),
                  pl.BlockSpec((D, tn), lambda j: (0, j)),
                  pl.BlockSpec((1, tn), lambda j: (0, j))],
        out_specs=pl.BlockSpec((rows, tn), lambda j: (0, j)),
        compiler_params=_cparams(("arbitrary",), 40),
        name="ada",
    )(c8, w_ada, b_ada.reshape(1, n_out))
    return out[:B].reshape(B, N_MOD, D)


def _router_topk(hf, wr_hi_ref, wr_lo_ref, br_ref):
    h_hi = hf.astype(BF16)
    h_lo = (hf - h_hi.astype(F32)).astype(BF16)
    logits = (jnp.dot(h_hi, wr_hi_ref[...], preferred_element_type=F32)
              + jnp.dot(h_lo, wr_hi_ref[...], preferred_element_type=F32)
              + jnp.dot(h_hi, wr_lo_ref[...], preferred_element_type=F32)) + br_ref[...]
    lane = lax.broadcasted_iota(I32, logits.shape, 1)
    lane_f = lane.astype(F32)
    l = jnp.where(lane < N_EXPERTS, logits, -jnp.inf)
    idx_out = jnp.zeros(logits.shape, F32)
    val_out = jnp.full(logits.shape, -jnp.inf, F32)
    m0 = None
    for k in range(TOP_K):
        m = jnp.max(l, axis=1, keepdims=True)
        am = jnp.min(jnp.where(l == m, lane_f, float(LANES)), axis=1, keepdims=True)
        idx_out = jnp.where(lane == k, am, idx_out)
        val_out = jnp.where(lane == k, m, val_out)
        l = jnp.where(lane_f == am, -jnp.inf, l)
        if k == 0:
            m0 = m
    e = jnp.where(lane < TOP_K, jnp.exp(val_out - m0), 0.0)
    gates = e / jnp.sum(e, axis=1, keepdims=True)
    return idx_out.astype(I32), gates


def _post_kernel(*refs, mode, nxt, i_gate, i_shift, i_scale):
    it = iter(refs)
    x_ref = next(it)
    if mode == "dense":
        y_ref = next(it)
    elif mode == "moe":
        y4_ref = next(it)
        g4_ref = next(it)
    cond_ref = next(it)
    tab_ref = next(it)
    tabn_ref = next(it)
    gpost_ref = next(it) if mode != "none" else None
    gpre_ref = next(it) if nxt != "none" else None
    if nxt == "router":
        wr_hi_ref, wr_lo_ref, br_ref = next(it), next(it), next(it)
    outs = list(it)

    mod = cond_ref[0] + tab_ref[...]
    modn = cond_ref[0] + tabn_ref[...]
    x = x_ref[...]
    if mode == "dense":
        y = y_ref[...].astype(F32)
    elif mode == "moe":
        g4 = g4_ref[...]
        y = None
        for k in range(TOP_K):
            lo, hi = _unpack_halves(_load_token_rows(y4_ref.at[k], x.shape[0]))
            yk = jnp.concatenate([lo.astype(F32), hi.astype(F32)], axis=1) * g4[:, k:k + 1]
            y = yk if y is None else y + yk
    o = 0
    if mode != "none":
        x = x + mod[i_gate:i_gate + 1, :] * _rms(y, gpost_ref[...])
        outs[o][...] = x
        o += 1
    if nxt != "none":
        hf = _rms(x, gpre_ref[...]) * (1.0 + modn[i_scale:i_scale + 1, :]) + modn[i_shift:i_shift + 1, :]
        if nxt == "plain":
            outs[o][...] = hf.astype(BF16)
        else:
            _store_token_rows(outs[o], _pack_halves(hf))
            idx, gates = _router_topk(hf, wr_hi_ref, wr_lo_ref, br_ref)
            outs[o + 1][...] = idx
            outs[o + 2][...] = gates


def _post(x, cond, tab, tab_next, *, S, mode, nxt, y=None, y4=None, g4=None, g_post=None, g_pre=None,
          router=None, i_gate=0, i_shift=0, i_scale=0, tm=256):
    N, D = x.shape
    nb = S // tm
    rpt = D // 2 // LANES
    row = lambda i: (i, 0)
    args, specs = [x], [pl.BlockSpec((tm, D), row)]
    if mode == "dense":
        args.append(y)
        specs.append(pl.BlockSpec((tm, D), row))
    elif mode == "moe":
        args += [y4, g4]
        specs += [pl.BlockSpec((TOP_K, tm * rpt, LANES), lambda i: (0, i, 0)), pl.BlockSpec((tm, LANES), row)]
    args += [cond, tab, tab_next]
    specs += [pl.BlockSpec((1, N_MOD, D), lambda i: (i // nb, 0, 0)), pl.BlockSpec((N_MOD, D), lambda i: (0, 0)),
              pl.BlockSpec((N_MOD, D), lambda i: (0, 0))]
    if mode != "none":
        args.append(g_post.reshape(1, D))
        specs.append(pl.BlockSpec((1, D), lambda i: (0, 0)))
    if nxt != "none":
        args.append(g_pre.reshape(1, D))
        specs.append(pl.BlockSpec((1, D), lambda i: (0, 0)))
    if nxt == "router":
        args += list(router)
        specs += [pl.BlockSpec((D, LANES), lambda i: (0, 0)), pl.BlockSpec((D, LANES), lambda i: (0, 0)),
                  pl.BlockSpec((1, LANES), lambda i: (0, 0))]
    out_shape, out_specs = [], []
    if mode != "none":
        out_shape.append(jax.ShapeDtypeStruct((N, D), F32))
        out_specs.append(pl.BlockSpec((tm, D), row))
    if nxt == "plain":
        out_shape.append(jax.ShapeDtypeStruct((N, D), BF16))
        out_specs.append(pl.BlockSpec((tm, D), row))
    elif nxt == "router":
        out_shape += [jax.ShapeDtypeStruct((N * rpt, LANES), U32), jax.ShapeDtypeStruct((N, LANES), I32),
                      jax.ShapeDtypeStruct((N, LANES), F32)]
        out_specs += [pl.BlockSpec((tm * rpt, LANES), row), pl.BlockSpec((tm, LANES), row), pl.BlockSpec((tm, LANES), row)]
    return pl.pallas_call(
        functools.partial(_post_kernel, mode=mode, nxt=nxt, i_gate=i_gate, i_shift=i_shift, i_scale=i_scale),
        out_shape=out_shape, grid=(N // tm,), in_specs=specs, out_specs=out_specs,
        compiler_params=_cparams(("arbitrary",), 56),
        name=f"post_{mode}_{nxt}",
    )(*args)


def _mm_kernel(*refs, n_a):
    a_refs, w_ref, o_ref = refs[:n_a], refs[n_a], refs[n_a + 1]
    acc, off = None, 0
    for a in a_refs:
        kg = a.shape[1]
        part = jnp.dot(a[...], w_ref[off:off + kg, :], preferred_element_type=F32)
        acc = part if acc is None else acc + part
        off += kg
    o_ref[...] = acc.astype(o_ref.dtype)


def _matmul(a_list, w, out_dtype, tm, tn):
    N = a_list[0].shape[0]
    K, n_out = w.shape
    specs = [pl.BlockSpec((tm, a.shape[1]), lambda i, j: (i, 0)) for a in a_list]
    specs.append(pl.BlockSpec((K, tn), lambda i, j: (0, j)))
    return pl.pallas_call(
        functools.partial(_mm_kernel, n_a=len(a_list)),
        out_shape=jax.ShapeDtypeStruct((N, n_out), out_dtype),
        grid=(N // tm, n_out // tn), in_specs=specs,
        out_specs=pl.BlockSpec((tm, tn), lambda i, j: (i, j)),
        compiler_params=_cparams(("arbitrary", "arbitrary"), 56),
        name="matmul",
    )(*a_list, w)


def _cast_kernel(x_ref, o_ref):
    o_ref[...] = x_ref[0].astype(o_ref.dtype)


def _layer_bf16(w, l):
    C = w.shape[-1]
    R = math.prod(w.shape[1:-1])
    tr = math.gcd(R, max(8, pl.next_power_of_2(CAST_BLOCK_BYTES // (4 * C) + 1) // 2))
    out = pl.pallas_call(
        _cast_kernel,
        out_shape=jax.ShapeDtypeStruct((R, C), BF16),
        grid=(R // tr,),
        in_specs=[pl.BlockSpec((1, tr, C), lambda i: (l, i, 0))],
        out_specs=pl.BlockSpec((tr, C), lambda i: (i, 0)),
        compiler_params=_cparams(("arbitrary",), 40),
        name="cast_bf16",
    )(w.reshape(w.shape[0], R, C))
    return out.reshape(w.shape[1:])


def _conv_kernel(val_ref, gate_ref, pval_ref, pgate_ref, w_ref, b_ref, lg_ref, lb_ref, o_ref, ubuf, ybuf, shbuf, *, ts):
    G = o_ref.shape[1]
    i = pl.program_id(1)
    v = val_ref[...].astype(F32)
    g = gate_ref[...].astype(F32)
    ubuf[CONV_HALO:CONV_HALO + ts, :] = v * jax.nn.sigmoid(g)
    pv = pval_ref[...].astype(F32)
    pg = pgate_ref[...].astype(F32)
    ubuf[0:CONV_HALO, :] = jnp.where(i > 0, pv * jax.nn.sigmoid(pg), 0.0)
    base = CONV_HALO - (CONV_WIDTH - 1)
    rs = 128
    span = ts + CONV_HALO - SUBLANES

    def chan(c, carry):
        c0 = pl.multiple_of(c * LANES, LANES)
        win = ubuf[:, pl.ds(c0, LANES)]
        for s in range(1, SUBLANES):
            shbuf[s - 1, 0:span, :] = win[s:s + span]
        for r0 in range(0, ts, rs):
            acc = jnp.zeros((rs, LANES), F32)
            for j in range(CONV_WIDTH):
                q, s = divmod(base + j, SUBLANES)
                a0 = r0 + SUBLANES * q
                src = ubuf[a0:a0 + rs, pl.ds(c0, LANES)] if s == 0 else shbuf[s - 1, a0:a0 + rs, :]
                acc = acc + w_ref[j:j + 1, pl.ds(c0, LANES)] * src
            ybuf[r0:r0 + rs, pl.ds(c0, LANES)] = acc
        return carry

    lax.fori_loop(0, G // LANES, chan, 0)
    for r0 in range(0, ts, 64):
        y = ybuf[r0:r0 + 64, :] + b_ref[...]
        mu = jnp.mean(y, axis=-1, keepdims=True)
        d = y - mu
        var = jnp.mean(d * d, axis=-1, keepdims=True)
        z = d * lax.rsqrt(var + LN_EPS) * lg_ref[...] + lb_ref[...]
        o_ref[r0:r0 + 64, :] = (z * jax.nn.sigmoid(z)).astype(o_ref.dtype)


def _conv_mixer(p, conv_w, conv_b, ln_g, ln_b, *, B, S, G, ts=256):
    nb = S // ts
    hb = ts // CONV_HALO
    wpad = jnp.zeros((CONV_HALO, G), F32).at[:CONV_WIDTH].set(conv_w)
    cur = lambda col: pl.BlockSpec((ts, G), lambda b, i: (b * nb + i, col))
    prev = lambda col: pl.BlockSpec((CONV_HALO, G), lambda b, i: (jnp.maximum((b * nb + i) * hb - 1, 0), col))
    vec = pl.BlockSpec((1, G), lambda b, i: (0, 0))
    return pl.pallas_call(
        functools.partial(_conv_kernel, ts=ts),
        out_shape=jax.ShapeDtypeStruct((B * S, G), BF16),
        grid=(B, nb),
        in_specs=[cur(0), cur(1), prev(0), prev(1), pl.BlockSpec((CONV_HALO, G), lambda b, i: (0, 0)), vec, vec, vec],
        out_specs=pl.BlockSpec((ts, G), lambda b, i: (b * nb + i, 0)),
        scratch_shapes=[pltpu.VMEM((ts + CONV_HALO, G), F32), pltpu.VMEM((ts, G), F32),
                        pltpu.VMEM((SUBLANES - 1, ts + CONV_HALO, LANES), F32)],
        compiler_params=_cparams(("arbitrary", "arbitrary"), 32),
        name="conv_mixer",
    )(p, p, p, p, wpad, conv_b.reshape(1, G), ln_g.reshape(1, G), ln_b.reshape(1, G))


_NT = (((1,), (1,)), ((), ()))
SB_HEADS, DIFF_HEADS, MOBA_HEADS = 4, 2, 4
SB_UNDERFLOW = 110.0
DIFF_TILE = 512
MOBA_GROUP = 4
CAST_BLOCK_BYTES = 8 * MIB
ROWCOPY_CHUNK = 2048
ISSUE_UNROLL = 8
WAIT_GROUP = 64
LOG2E = 1.4426950408889634


def _hs(hh):
    return slice(hh * HEAD_DIM, (hh + 1) * HEAD_DIM)


def _sb_kernel(q_ref, k_ref, v_ref, o_ref, *, t, nh):
    i = pl.program_id(2)
    qs = [q_ref[:, _hs(hh)] for hh in range(nh)]
    scale = HEAD_DIM ** -0.5
    row = lax.broadcasted_iota(I32, (t, t), 0)
    col = lax.broadcasted_iota(I32, (t, t), 1)
    tri = jnp.where(row >= col, 1.0, 0.0).astype(BF16)
    past = col < row

    def block(j, st, diag):
        j0 = pl.multiple_of(j * t, t)
        zs = [lax.dot_general(qs[hh], k_ref[pl.ds(j0, t), _hs(hh)], _NT, preferred_element_type=F32) * scale
              for hh in range(nh)]
        his, los = [], []
        for hh in range(nh):
            z = zs[hh]
            sp = jnp.maximum(z, 0.0) + jnp.log(1.0 + jnp.exp(-jnp.abs(z)))
            u = jnp.where(past, sp, 0.0) if diag else sp
            u_hi = u.astype(BF16)
            his.append(u_hi)
            los.append((u - u_hi.astype(F32)).astype(BF16))
        tls = [jnp.dot(his[hh], tri, preferred_element_type=F32) + jnp.dot(los[hh], tri, preferred_element_type=F32)
               for hh in range(nh)]
        a_s = []
        for hh in range(nh):
            e = zs[hh] - (tls[hh] + st[hh][0])
            if diag:
                e = jnp.where(past, e, -jnp.inf)
            a_s.append(jnp.exp(e).astype(BF16))
        return tuple((st[hh][0] + tls[hh][:, 0:1],
                      st[hh][1] + jnp.dot(a_s[hh], v_ref[pl.ds(j0, t), _hs(hh)], preferred_element_type=F32))
                     for hh in range(nh))

    def live(st):
        lo = st[0][0]
        for hh in range(1, nh):
            lo = jnp.minimum(lo, st[hh][0])
        return (jnp.min(lo) < SB_UNDERFLOW).astype(I32)

    def body(c):
        st = block(i - 1 - c[0], c[2], False)
        return c[0] + 1, live(st), st

    init = tuple((jnp.zeros((t, 1), F32), jnp.zeros((t, HEAD_DIM), F32)) for _ in range(nh))
    st = block(i, init, True)
    _, _, st = lax.while_loop(lambda c: (c[0] < i) & (c[1] > 0), body, (jnp.int32(0), live(st), st))
    for hh in range(nh):
        o_ref[:, _hs(hh)] = st[hh][1].astype(o_ref.dtype)


def _diff_kernel(q_ref, k_ref, v_ref, dl_ref, g_ref, o_ref, *, t, nh, lam_init):
    i = pl.program_id(2)
    scale = DIFF_QK_DIM ** -0.5
    assert math.frexp(scale)[0] == 0.5
    lane = lax.broadcasted_iota(I32, (t, HEAD_DIM), 1)
    qs = []
    for hh in range(nh):
        q = q_ref[:, _hs(hh)] * scale
        zero = jnp.zeros_like(q)
        qs.append((jnp.where(lane < DIFF_QK_DIM, q, zero), jnp.where(lane >= DIFF_QK_DIM, q, zero)))
    row = lax.broadcasted_iota(I32, (t, t), 0)
    col = lax.broadcasted_iota(I32, (t, t), 1)
    causal = col <= row
    dl = dl_ref[...]
    lam = (jnp.exp(jnp.sum(dl[0:1] * dl[1:2], axis=1, keepdims=True))
           - jnp.exp(jnp.sum(dl[2:3] * dl[3:4], axis=1, keepdims=True)) + lam_init)

    def block(j, st, diag):
        j0 = pl.multiple_of(j * t, t)
        ss = [lax.dot_general(qs[hh][mi], k_ref[pl.ds(j0, t), _hs(hh)], _NT, preferred_element_type=F32)
              for hh in range(nh) for mi in range(2)]
        prs, alphas, ls, ms = [], [], [], []
        for ci in range(2 * nh):
            m, l, acc = st[ci]
            s = jnp.where(causal, ss[ci], NEG_BIG) if diag else ss[ci]
            m_new = jnp.maximum(m, jnp.max(s, axis=1, keepdims=True))
            alpha = jnp.exp2((m - m_new) * LOG2E)
            pr = jnp.exp2((s - m_new) * LOG2E)
            ls.append(alpha * l + jnp.sum(pr, axis=1, keepdims=True))
            prs.append(pr.astype(BF16))
            alphas.append(alpha)
            ms.append(m_new)
        new = []
        for ci in range(2 * nh):
            vj = v_ref[pl.ds(j0, t), _hs(ci // 2)]
            acc = alphas[ci] * st[ci][2] + jnp.dot(prs[ci], vj, preferred_element_type=F32)
            new.append((ms[ci], ls[ci], acc))
        return tuple(new)

    init = tuple((jnp.full((t, 1), NEG_BIG, F32), jnp.zeros((t, 1), F32), jnp.zeros((t, HEAD_DIM), F32))
                 for _ in range(2 * nh))
    st = block(i, init, True)
    st = lax.fori_loop(0, i, lambda j, s: block(j, s, False), st)
    for hh in range(nh):
        (_, l0, a0), (_, l1, a1) = st[2 * hh], st[2 * hh + 1]
        o = a0 / l0 - lam * (a1 / l1)
        o_ref[:, _hs(hh)] = (_rms(o, g_ref[...]) * (1.0 - lam_init)).astype(o_ref.dtype)


def _moba_kernel(q_ref, k_ref, v_ref, o_ref, km_hi, km_lo, *, t, nh, nkb, grp):
    own = pl.program_id(2)
    scale = HEAD_DIM ** -0.5
    topk = min(MOBA_TOPK, nkb)

    @pl.when(own == 0)
    def _():
        km_hi[...] = jnp.zeros_like(km_hi)
        km_lo[...] = jnp.zeros_like(km_lo)
        for hh in range(nh):
            for blk in range(nkb):
                km = jnp.mean(k_ref[blk * t:(blk + 1) * t, _hs(hh)].astype(F32), axis=0, keepdims=True)
                hi = km.astype(BF16)
                km_hi[hh, blk:blk + 1, :] = hi
                km_lo[hh, blk:blk + 1, :] = (km - hi.astype(F32)).astype(BF16)

    lane = lax.broadcasted_iota(I32, (t, LANES), 1)
    lane_f = lane.astype(F32)
    row = lax.broadcasted_iota(I32, (t, t), 0)
    col = lax.broadcasted_iota(I32, (t, t), 1)
    qs, sels = [], []
    for hh in range(nh):
        q = q_ref[:, _hs(hh)]
        gate = (lax.dot_general(q, km_hi[hh], _NT, preferred_element_type=F32)
                + lax.dot_general(q, km_lo[hh], _NT, preferred_element_type=F32))
        cand = jnp.where(lane < own, gate, -jnp.inf)
        sel = jnp.zeros(gate.shape, F32)
        for _ in range(topk):
            best = jnp.max(cand, axis=1, keepdims=True)
            first = jnp.min(jnp.where(cand == best, lane_f, float(LANES)), axis=1, keepdims=True)
            hit = (lane_f == first) & (best > -jnp.inf)
            sel = jnp.where(hit, 1.0, sel)
            cand = jnp.where(hit, -jnp.inf, cand)
        qs.append(q)
        sels.append(sel)

    def attend(j, nblk, st, masks):
        j0 = pl.multiple_of(j * t, t)
        c = scale * LOG2E
        ss = [lax.dot_general(qs[hh], k_ref[pl.ds(j0, nblk * t), _hs(hh)], _NT, preferred_element_type=F32)
              for hh in range(nh)]
        prs, alphas, ls, ms = [], [], [], []
        for hh in range(nh):
            m, l, acc = st[hh]
            s = ss[hh]
            s = jnp.concatenate([jnp.where(masks[hh][b], s[:, b * t:(b + 1) * t], NEG_BIG) for b in range(nblk)], axis=1)
            m_new = jnp.maximum(m, jnp.max(s, axis=1, keepdims=True))
            alpha = jnp.exp2((m - m_new) * c)
            pr = jnp.exp2((s - m_new) * c)
            ls.append(alpha * l + jnp.sum(pr, axis=1, keepdims=True))
            prs.append(pr.astype(BF16))
            alphas.append(alpha)
            ms.append(m_new)
        new = []
        for hh in range(nh):
            acc = alphas[hh] * st[hh][2] + jnp.dot(prs[hh], v_ref[pl.ds(j0, nblk * t), _hs(hh)], preferred_element_type=F32)
            new.append((ms[hh], ls[hh], acc))
        return tuple(new)

    init = tuple((jnp.full((t, 1), NEG_BIG, F32), jnp.zeros((t, 1), F32), jnp.zeros((t, HEAD_DIM), F32))
                 for _ in range(nh))
    st = attend(own, 1, init, [[col <= row]] * nh)

    def body(g, st):
        picked = [[jnp.max(jnp.where(lane == g * grp + b, sels[hh], 0.0), axis=1, keepdims=True) > 0.5
                   for b in range(grp)] for hh in range(nh)]
        return attend(g * grp, grp, st, picked)

    st = lax.fori_loop(0, (own + grp - 1) // grp, body, st)
    for hh in range(nh):
        _, l, acc = st[hh]
        o_ref[:, _hs(hh)] = (acc / l).astype(o_ref.dtype)


def _attention(kernel_fn, p, extra, extra_specs, scratch, *, B, S, G, c_q, c_k, c_v, t, nh, name):
    H = G // HEAD_DIM
    nq = S // t
    w = nh * HEAD_DIM
    assert H % nh == 0 and c_q % nh == 0 and c_k % nh == 0 and c_v % nh == 0
    return pl.pallas_call(
        kernel_fn,
        out_shape=jax.ShapeDtypeStruct((B * S, G), BF16),
        grid=(B, H // nh, nq),
        in_specs=[pl.BlockSpec((t, w), lambda b, h, i: (b * nq + i, c_q // nh + h)),
                  pl.BlockSpec((S, w), lambda b, h, i: (b, c_k // nh + h)),
                  pl.BlockSpec((S, w), lambda b, h, i: (b, c_v // nh + h))] + extra_specs,
        out_specs=pl.BlockSpec((t, w), lambda b, h, i: (b * nq + i, h)),
        scratch_shapes=scratch,
        compiler_params=_cparams(("arbitrary", "arbitrary", "arbitrary"), 40),
        name=name,
    )(p, p, p, *extra)


def _moe_plan(top_idx, N, tm):
    E, NK = N_EXPERTS, N * TOP_K
    nblocks = (NK + E * (tm - 1) + tm - 1) // tm
    e_flat = top_idx.reshape(NK)
    order = jnp.argsort(e_flat, stable=True).astype(I32)
    counts = jnp.sum((e_flat[:, None] == jnp.arange(E, dtype=I32)[None, :]).astype(I32), axis=0)
    starts = jnp.cumsum(counts) - counts
    nblk_e = (counts + tm - 1) // tm
    bend = jnp.cumsum(nblk_e)
    bstart = bend - nblk_e
    b = jnp.arange(nblocks, dtype=I32)
    blk_e = jnp.minimum(jnp.sum((bend[None, :] <= b[:, None]).astype(I32), axis=1), E - 1).astype(I32)
    blk_off = (b - bstart[blk_e]) * tm
    blk_nv = jnp.where(b < bend[-1], jnp.clip(counts[blk_e] - blk_off, 0, tm), 0).astype(I32)
    ri = jnp.arange(tm, dtype=I32)
    valid = ri[None, :] < blk_nv[:, None]
    src = jnp.clip(starts[blk_e][:, None] + blk_off[:, None] + ri[None, :], 0, NK - 1)
    flat = order[src]
    row_tok = jnp.where(valid, flat // TOP_K, 0).astype(I32)
    row_slot = jnp.where(valid, (flat % TOP_K) * N + flat // TOP_K, -1).astype(I32)
    return blk_e, blk_nv, row_tok, row_slot


def _rowcopy_kernel(cnt_ref, idx_hbm, src, dst, idx_smem, isem, csem, *, ch, rpt, gather):
    s = pl.program_id(0)
    ns = pl.num_programs(0)
    slot = s % 2

    def idx_copy(step, sl):
        return pltpu.make_async_copy(idx_hbm.at[step], idx_smem.at[sl], isem.at[sl])

    @pl.when(s == 0)
    def _():
        idx_copy(0, 0).start()

    idx_copy(s, slot).wait()

    @pl.when(s + 1 < ns)
    def _():
        idx_copy(s + 1, 1 - slot).start()

    def row_copy(r, t):
        blk = pl.ds(pl.multiple_of(r * rpt, rpt), rpt)
        far = pl.ds(pl.multiple_of(t * rpt, rpt), rpt)
        if gather:
            return pltpu.make_async_copy(src.at[far], dst.at[blk], csem)
        return pltpu.make_async_copy(src.at[blk], dst.at[far], csem)

    def issue(g, c):
        for u in range(ISSUE_UNROLL):
            r = g * ISSUE_UNROLL + u
            t = idx_smem[slot, r]
            if gather:
                row_copy(r, t).start(priority=u % 2)
            else:
                @pl.when(t >= 0)
                def _():
                    row_copy(r, t).start(priority=u % 2)
        return c

    def drain_group(g, c):
        for _ in range(WAIT_GROUP):
            row_copy(0, 0).wait()
        return c

    def drain_one(r, c):
        row_copy(0, 0).wait()
        return c

    lax.fori_loop(0, ch // ISSUE_UNROLL, issue, 0)
    n = cnt_ref[s]
    lax.fori_loop(0, n // WAIT_GROUP, drain_group, 0)
    lax.fori_loop(0, n % WAIT_GROUP, drain_one, 0)


def _rowcopy(idx, src, n_dst_tokens, *, rpt, gather, ch=ROWCOPY_CHUNK):
    ch = math.gcd(ch, idx.shape[0])
    ns = idx.shape[0] // ch
    idx2 = idx.reshape(ns, ch)
    cnt = jnp.sum((idx2 >= 0).astype(I32), axis=1)
    blk = pl.BlockSpec((ch * rpt, LANES), lambda s, cnt: (s, 0))
    hbm = pl.BlockSpec(memory_space=pl.ANY)
    return pl.pallas_call(
        functools.partial(_rowcopy_kernel, ch=ch, rpt=rpt, gather=gather),
        out_shape=jax.ShapeDtypeStruct((n_dst_tokens * rpt, LANES), src.dtype),
        grid_spec=pltpu.PrefetchScalarGridSpec(
            num_scalar_prefetch=1, grid=(ns,),
            in_specs=[hbm, hbm if gather else blk],
            out_specs=blk if gather else hbm,
            scratch_shapes=[pltpu.SMEM((2, ch), I32), pltpu.SemaphoreType.DMA((2,)), pltpu.SemaphoreType.DMA]),
        compiler_params=_cparams(("arbitrary",), 48),
        name="row_gather" if gather else "row_scatter",
    )(cnt, idx2, src)


def _expert_kernel(be_ref, nv_ref, x_ref, w1_ref, b1_ref, w2_ref, b2_ref, o_ref, *, tm):
    b = pl.program_id(0)
    F = w2_ref.shape[1]
    half = w1_ref.shape[1] // 2

    @pl.when(nv_ref[b] > 0)
    def _():
        lo, hi = _unpack_halves(_load_token_rows(x_ref, tm))
        a = (jnp.dot(lo, w1_ref[0, :half, :], preferred_element_type=F32)
             + jnp.dot(hi, w1_ref[0, half:, :], preferred_element_type=F32) + b1_ref[0])
        g = jnp.minimum(a[:, :F], SWIGLU_LIMIT)
        lin = jnp.clip(a[:, F:], -SWIGLU_LIMIT, SWIGLU_LIMIT)
        act = g * jax.nn.sigmoid(SWIGLU_ALPHA * g) * (lin + 1.0)
        y = jnp.dot(act.astype(BF16), w2_ref[0], preferred_element_type=F32) + b2_ref[0]
        _store_token_rows(o_ref, _pack_halves(y))

    @pl.when(nv_ref[b] == 0)
    def _():
        o_ref[...] = jnp.zeros_like(o_ref)


def _experts(blk_e, blk_nv, xs, w1, b1, w2, b2, *, tm, rpt):
    E, D, F2 = w1.shape
    F = F2 // 2
    nblocks = xs.shape[0] // (tm * rpt)
    return pl.pallas_call(
        functools.partial(_expert_kernel, tm=tm),
        out_shape=jax.ShapeDtypeStruct(xs.shape, U32),
        grid_spec=pltpu.PrefetchScalarGridSpec(
            num_scalar_prefetch=2, grid=(nblocks,),
            in_specs=[pl.BlockSpec((tm * rpt, LANES), lambda b, be, nv: (b, 0)),
                      pl.BlockSpec((1, D, F2), lambda b, be, nv: (be[b], 0, 0)),
                      pl.BlockSpec((1, 1, F2), lambda b, be, nv: (be[b], 0, 0)),
                      pl.BlockSpec((1, F, D), lambda b, be, nv: (be[b], 0, 0)),
                      pl.BlockSpec((1, 1, D), lambda b, be, nv: (be[b], 0, 0))],
            out_specs=pl.BlockSpec((tm * rpt, LANES), lambda b, be, nv: (b, 0))),
        compiler_params=_cparams(("arbitrary",), 56),
        name="experts",
    )(blk_e, blk_nv, xs, w1, b1.reshape(E, 1, F2), w2, b2.reshape(E, 1, D))


def _moe(hpk, top_idx, w1, b1, w2, b2, *, N, rpt, tm=256):
    blk_e, blk_nv, row_tok, row_slot = _moe_plan(top_idx, N, tm)
    xs = _rowcopy(row_tok.reshape(-1), hpk, row_tok.size, rpt=rpt, gather=True)
    ys = _experts(blk_e, blk_nv, xs, w1, b1, w2, b2, tm=tm, rpt=rpt)
    return _rowcopy(row_slot.reshape(-1), ys, TOP_K * N, rpt=rpt, gather=False)


def kernel(x, c, w_ada, b_ada, ada_table, norm_pre, norm_post, w_in, w_out, conv_w, conv_b, conv_ln_g, conv_ln_b,
           diff_lambda, diff_norm_g, w_router, b_router, w1, b1, w2, b2):
    B, S, D = x.shape
    depth = w_in.shape[0]
    N = B * S
    G = D // N_GROUPS
    H = G // HEAD_DIM
    E = w_router.shape[2]
    assert E == N_EXPERTS and S % MOBA_BLOCK == 0 and G % HEAD_DIM == 0
    nkb = S // MOBA_BLOCK

    cond = _ada(c, w_ada, b_ada)
    xf = x.reshape(N, D)
    h = None
    for l in range(depth):
        lam_init = 0.8 - 0.6 * math.exp(-0.3 * l)
        tab = ada_table[l]
        if l == 0:
            (h,) = _post(xf, cond, tab, tab, S=S, mode="none", nxt="plain", g_pre=norm_pre[l, 0], i_shift=0, i_scale=1)
        p = _matmul([h], _layer_bf16(w_in, l), BF16, 1024, 512 if w_in.shape[2] % 512 == 0 else G)

        y_conv = _conv_mixer(p, conv_w[l], conv_b[l], conv_ln_g[l], conv_ln_b[l], B=B, S=S, G=G)
        t = MOBA_BLOCK
        nh_sb, nh_diff, nh_moba = (math.gcd(n, H) for n in (SB_HEADS, DIFF_HEADS, MOBA_HEADS))
        td = math.gcd(DIFF_TILE, S)
        y_sb = _attention(functools.partial(_sb_kernel, t=t, nh=nh_sb), p, [], [], [], B=B, S=S, G=G,
                          c_q=2 * H, c_k=3 * H, c_v=4 * H, t=t, nh=nh_sb, name="stick_breaking")
        y_diff = _attention(
            functools.partial(_diff_kernel, t=td, nh=nh_diff, lam_init=lam_init), p,
            [diff_lambda[l], diff_norm_g[l].reshape(1, HEAD_DIM)],
            [pl.BlockSpec((4, DIFF_QK_DIM), lambda b, h_, i: (0, 0)), pl.BlockSpec((1, HEAD_DIM), lambda b, h_, i: (0, 0))],
            [], B=B, S=S, G=G, c_q=5 * H, c_k=6 * H, c_v=7 * H, t=td, nh=nh_diff, name="diff_attention")
        y_moba = _attention(
            functools.partial(_moba_kernel, t=t, nh=nh_moba, nkb=nkb, grp=math.gcd(MOBA_GROUP, nkb)), p, [], [],
            [pltpu.VMEM((nh_moba, LANES, HEAD_DIM), BF16), pltpu.VMEM((nh_moba, LANES, HEAD_DIM), BF16)],
            B=B, S=S, G=G, c_q=8 * H, c_k=9 * H, c_v=10 * H, t=t, nh=nh_moba, name="moba")

        y = _matmul([y_conv, y_sb, y_diff, y_moba], _layer_bf16(w_out, l), BF16, 1024, 512)

        wr = jnp.zeros((D, LANES), F32).at[:, :E].set(w_router[l])
        wr_hi = wr.astype(BF16)
        wr_lo = (wr - wr_hi.astype(F32)).astype(BF16)
        br = jnp.zeros((1, LANES), F32).at[0, :E].set(b_router[l])
        xf, hpk, top_idx, gates = _post(xf, cond, tab, tab, S=S, mode="dense", nxt="router", y=y, g_post=norm_post[l, 0],
                                        g_pre=norm_pre[l, 1], router=(wr_hi, wr_lo, br), i_gate=2, i_shift=3, i_scale=4)

        rpt = D // 2 // LANES
        y4 = _moe(hpk, top_idx[:, :TOP_K], _layer_bf16(w1, l), b1[l], _layer_bf16(w2, l), b2[l], N=N, rpt=rpt)
        y4 = y4.reshape(TOP_K, N * rpt, LANES)
        if l + 1 < depth:
            xf, h = _post(xf, cond, tab, ada_table[l + 1], S=S, mode="moe", nxt="plain", y4=y4, g4=gates,
                          g_post=norm_post[l, 1], g_pre=norm_pre[l + 1, 0], i_gate=5, i_shift=0, i_scale=1)
        else:
            (xf,) = _post(xf, cond, tab, tab, S=S, mode="moe", nxt="none", y4=y4, g4=gates, g_post=norm_post[l, 1], i_gate=5)
    return xf.reshape(B, S, D)
```

```python
import functools
import math

import jax
import jax.numpy as jnp
from jax import lax
from jax.experimental import pallas as pl
from jax.experimental.pallas import tpu as pltpu

F32 = jnp.float32
BF16 = jnp.bfloat16
U32 = jnp.uint32
I32 = jnp.int32

N_GROUPS = 4
HEAD_DIM = 128
DIFF_QK_DIM = HEAD_DIM // 2
CONV_WIDTH = 31
MOBA_BLOCK = 256
MOBA_TOPK = 3
N_EXPERTS = 32
TOP_K = 4
SWIGLU_ALPHA = 1.702
SWIGLU_LIMIT = 7.0
RMS_EPS = 1e-6
LN_EPS = 1e-5
N_MOD = 6
LANES = 128
SUBLANES = 8
CONV_HALO = 32
NEG_BIG = -1e30
MIB = 1024 * 1024


def _cparams(sem, vmem_mib):
    return pltpu.CompilerParams(dimension_semantics=sem, vmem_limit_bytes=vmem_mib * MIB)


def _rms(xf, g):
    return xf * lax.rsqrt(jnp.mean(xf * xf, axis=-1, keepdims=True) + RMS_EPS) * g


def _pack_halves(a):
    k = a.shape[1] // 2
    lo = lax.bitcast_convert_type(a[:, :k].astype(BF16).astype(F32), U32)
    hi = lax.bitcast_convert_type(a[:, k:].astype(BF16).astype(F32), U32)
    return (lo >> 16) | (hi & jnp.uint32(0xFFFF0000))


def _unpack_halves(w):
    lo = lax.bitcast_convert_type(w << 16, F32).astype(BF16)
    hi = lax.bitcast_convert_type(w & jnp.uint32(0xFFFF0000), F32).astype(BF16)
    return lo, hi


def _unpack_halves_f32(w):
    return lax.bitcast_convert_type(w << 16, F32), lax.bitcast_convert_type(w & jnp.uint32(0xFFFF0000), F32)


def _store_token_rows(ref, words):
    m, rpt = words.shape[0], words.shape[1] // LANES
    for c in range(rpt):
        ref[pl.ds(c, m, stride=rpt), :] = words[:, c * LANES:(c + 1) * LANES]


def _load_token_rows(ref, m):
    rpt = ref.shape[0] // m
    return jnp.concatenate([ref[pl.ds(c, m, stride=rpt), :] for c in range(rpt)], axis=1)


def _ada_kernel(c_ref, w_ref, b_ref, o_ref):
    c = c_ref[...]
    a = (c * jax.nn.sigmoid(c)).astype(BF16)
    o_ref[...] = jnp.dot(a, w_ref[...].astype(BF16), preferred_element_type=F32) + b_ref[...]


def _ada(c, w_ada, b_ada):
    B, D = c.shape
    n_out = w_ada.shape[1]
    tn = 512
    rows = 8
    c8 = jnp.zeros((rows, D), F32).at[:B].set(c)
    out = pl.pallas_call(
        _ada_kernel,
        out_shape=jax.ShapeDtypeStruct((rows, n_out), F32),
        grid=(n_out // tn,),
        in_specs=[pl.BlockSpec((rows, D), lambda j: (0, 0)),
                  pl.BlockSpec((D, tn), lambda j: (0, j)),
                  pl.BlockSpec((1, tn), lambda j: (0, j))],
        out_specs=pl.BlockSpec((rows, tn), lambda j: (0, j)),
        compiler_params=_cparams(("arbitrary",), 40),
        name="ada",
    )(c8, w_ada, b_ada.reshape(1, n_out))
    return out[:B].reshape(B, N_MOD, D)


def _router_topk(hf, wr_hi_ref, wr_lo_ref, br_ref):
    h_hi = hf.astype(BF16)
    h_lo = (hf - h_hi.astype(F32)).astype(BF16)
    logits = (jnp.dot(h_hi, wr_hi_ref[...], preferred_element_type=F32)
              + jnp.dot(h_lo, wr_hi_ref[...], preferred_element_type=F32)
              + jnp.dot(h_hi, wr_lo_ref[...], preferred_element_type=F32)) + br_ref[...]
    lane = lax.broadcasted_iota(I32, logits.shape, 1)
    lane_f = lane.astype(F32)
    l = jnp.where(lane < N_EXPERTS, logits, -jnp.inf)
    idx_out = jnp.zeros(logits.shape, F32)
    val_out = jnp.full(logits.shape, -jnp.inf, F32)
    m0 = None
    for k in range(TOP_K):
        m = jnp.max(l, axis=1, keepdims=True)
        am = jnp.min(jnp.where(l == m, lane_f, float(LANES)), axis=1, keepdims=True)
        idx_out = jnp.where(lane == k, am, idx_out)
        val_out = jnp.where(lane == k, m, val_out)
        l = jnp.where(lane_f == am, -jnp.inf, l)
        if k == 0:
            m0 = m
    e = jnp.where(lane < TOP_K, jnp.exp(val_out - m0), 0.0)
    gates = e / jnp.sum(e, axis=1, keepdims=True)
    return idx_out.astype(I32), gates


def _post_kernel(*refs, mode, nxt, i_gate, i_shift, i_scale):
    it = iter(refs)
    x_ref = next(it)
    if mode == "dense":
        y_ref = next(it)
    elif mode == "moe":
        y4_ref = next(it)
        g4_ref = next(it)
    cond_ref = next(it)
    tab_ref = next(it)
    tabn_ref = next(it)
    gpost_ref = next(it) if mode != "none" else None
    gpre_ref = next(it) if nxt != "none" else None
    if nxt == "router":
        wr_hi_ref, wr_lo_ref, br_ref = next(it), next(it), next(it)
    outs = list(it)

    mod = cond_ref[0] + tab_ref[...]
    modn = cond_ref[0] + tabn_ref[...]
    x = x_ref[...]
    if mode == "dense":
        y = y_ref[...].astype(F32)
    elif mode == "moe":
        g4 = g4_ref[...]
        y = None
        for k in range(TOP_K):
            lo, hi = _unpack_halves_f32(_load_token_rows(y4_ref.at[k], x.shape[0]))
            yk = jnp.concatenate([lo, hi], axis=1) * g4[:, k:k + 1]
            y = yk if y is None else y + yk
    o = 0
    if mode != "none":
        x = x + mod[i_gate:i_gate + 1, :] * _rms(y, gpost_ref[...])
        outs[o][...] = x
        o += 1
    if nxt != "none":
        hf = _rms(x, gpre_ref[...]) * (1.0 + modn[i_scale:i_scale + 1, :]) + modn[i_shift:i_shift + 1, :]
        if nxt == "plain":
            outs[o][...] = hf.astype(BF16)
        else:
            _store_token_rows(outs[o], _pack_halves(hf))
            idx, gates = _router_topk(hf, wr_hi_ref, wr_lo_ref, br_ref)
            outs[o + 1][...] = idx
            outs[o + 2][...] = gates


def _post(x, cond, tab, tab_next, *, S, mode, nxt, y=None, y4=None, g4=None, g_post=None, g_pre=None,
          router=None, i_gate=0, i_shift=0, i_scale=0, tm=256):
    N, D = x.shape
    nb = S // tm
    rpt = D // 2 // LANES
    row = lambda i: (i, 0)
    args, specs = [x], [pl.BlockSpec((tm, D), row)]
    if mode == "dense":
        args.append(y)
        specs.append(pl.BlockSpec((tm, D), row))
    elif mode == "moe":
        args += [y4, g4]
        specs += [pl.BlockSpec((TOP_K, tm * rpt, LANES), lambda i: (0, i, 0)), pl.BlockSpec((tm, LANES), row)]
    args += [cond, tab, tab_next]
    specs += [pl.BlockSpec((1, N_MOD, D), lambda i: (i // nb, 0, 0)), pl.BlockSpec((N_MOD, D), lambda i: (0, 0)),
              pl.BlockSpec((N_MOD, D), lambda i: (0, 0))]
    if mode != "none":
        args.append(g_post.reshape(1, D))
        specs.append(pl.BlockSpec((1, D), lambda i: (0, 0)))
    if nxt != "none":
        args.append(g_pre.reshape(1, D))
        specs.append(pl.BlockSpec((1, D), lambda i: (0, 0)))
    if nxt == "router":
        args += list(router)
        specs += [pl.BlockSpec((D, LANES), lambda i: (0, 0)), pl.BlockSpec((D, LANES), lambda i: (0, 0)),
                  pl.BlockSpec((1, LANES), lambda i: (0, 0))]
    out_shape, out_specs = [], []
    if mode != "none":
        out_shape.append(jax.ShapeDtypeStruct((N, D), F32))
        out_specs.append(pl.BlockSpec((tm, D), row))
    if nxt == "plain":
        out_shape.append(jax.ShapeDtypeStruct((N, D), BF16))
        out_specs.append(pl.BlockSpec((tm, D), row))
    elif nxt == "router":
        out_shape += [jax.ShapeDtypeStruct((N * rpt, LANES), U32), jax.ShapeDtypeStruct((N, LANES), I32),
                      jax.ShapeDtypeStruct((N, LANES), F32)]
        out_specs += [pl.BlockSpec((tm * rpt, LANES), row), pl.BlockSpec((tm, LANES), row), pl.BlockSpec((tm, LANES), row)]
    return pl.pallas_call(
        functools.partial(_post_kernel, mode=mode, nxt=nxt, i_gate=i_gate, i_shift=i_shift, i_scale=i_scale),
        out_shape=out_shape, grid=(N // tm,), in_specs=specs, out_specs=out_specs,
        compiler_params=_cparams(("arbitrary",), 56),
        name=f"post_{mode}_{nxt}",
    )(*args)


def _mm_kernel(*refs, n_a):
    a_refs, w_ref, o_ref = refs[:n_a], refs[n_a], refs[n_a + 1]
    acc, off = None, 0
    for a in a_refs:
        kg = a.shape[1]
        part = jnp.dot(a[...], w_ref[off:off + kg, :], preferred_element_type=F32)
        acc = part if acc is None else acc + part
        off += kg
    o_ref[...] = acc.astype(o_ref.dtype)


def _matmul(a_list, w, out_dtype, tm, tn):
    N = a_list[0].shape[0]
    K, n_out = w.shape
    specs = [pl.BlockSpec((tm, a.shape[1]), lambda i, j: (i, 0)) for a in a_list]
    specs.append(pl.BlockSpec((K, tn), lambda i, j: (0, j)))
    return pl.pallas_call(
        functools.partial(_mm_kernel, n_a=len(a_list)),
        out_shape=jax.ShapeDtypeStruct((N, n_out), out_dtype),
        grid=(N // tm, n_out // tn), in_specs=specs,
        out_specs=pl.BlockSpec((tm, tn), lambda i, j: (i, j)),
        compiler_params=_cparams(("arbitrary", "arbitrary"), 56),
        name="matmul",
    )(*a_list, w)


def _cast_kernel(x_ref, o_ref):
    o_ref[...] = x_ref[0].astype(o_ref.dtype)


def _layer_bf16(w, l):
    C = w.shape[-1]
    R = math.prod(w.shape[1:-1])
    tr = math.gcd(R, max(8, pl.next_power_of_2(CAST_BLOCK_BYTES // (4 * C) + 1) // 2))
    out = pl.pallas_call(
        _cast_kernel,
        out_shape=jax.ShapeDtypeStruct((R, C), BF16),
        grid=(R // tr,),
        in_specs=[pl.BlockSpec((1, tr, C), lambda i: (l, i, 0))],
        out_specs=pl.BlockSpec((tr, C), lambda i: (i, 0)),
        compiler_params=_cparams(("arbitrary",), 40),
        name="cast_bf16",
    )(w.reshape(w.shape[0], R, C))
    return out.reshape(w.shape[1:])


def _conv_kernel(val_ref, gate_ref, pval_ref, pgate_ref, w_ref, b_ref, lg_ref, lb_ref, o_ref, ubuf, ybuf, shbuf, *, ts):
    G = o_ref.shape[1]
    i = pl.program_id(1)
    v = val_ref[...].astype(F32)
    g = gate_ref[...].astype(F32)
    ubuf[CONV_HALO:CONV_HALO + ts, :] = v * jax.nn.sigmoid(g)
    pv = pval_ref[...].astype(F32)
    pg = pgate_ref[...].astype(F32)
    ubuf[0:CONV_HALO, :] = jnp.where(i > 0, pv * jax.nn.sigmoid(pg), 0.0)
    base = CONV_HALO - (CONV_WIDTH - 1)
    rs = 128
    span = ts + CONV_HALO - SUBLANES

    def chan(c, carry):
        c0 = pl.multiple_of(c * LANES, LANES)
        win = ubuf[:, pl.ds(c0, LANES)]
        for s in range(1, SUBLANES):
            shbuf[s - 1, 0:span, :] = win[s:s + span]
        for r0 in range(0, ts, rs):
            acc = jnp.zeros((rs, LANES), F32)
            for j in range(CONV_WIDTH):
                q, s = divmod(base + j, SUBLANES)
                a0 = r0 + SUBLANES * q
                src = ubuf[a0:a0 + rs, pl.ds(c0, LANES)] if s == 0 else shbuf[s - 1, a0:a0 + rs, :]
                acc = acc + w_ref[j:j + 1, pl.ds(c0, LANES)] * src
            ybuf[r0:r0 + rs, pl.ds(c0, LANES)] = acc
        return carry

    lax.fori_loop(0, G // LANES, chan, 0)
    for r0 in range(0, ts, 64):
        y = ybuf[r0:r0 + 64, :] + b_ref[...]
        mu = jnp.mean(y, axis=-1, keepdims=True)
        d = y - mu
        var = jnp.mean(d * d, axis=-1, keepdims=True)
        z = d * lax.rsqrt(var + LN_EPS) * lg_ref[...] + lb_ref[...]
        o_ref[r0:r0 + 64, :] = (z * jax.nn.sigmoid(z)).astype(o_ref.dtype)


def _conv_mixer(p, conv_w, conv_b, ln_g, ln_b, *, B, S, G, ts=256):
    nb = S // ts
    hb = ts // CONV_HALO
    wpad = jnp.zeros((CONV_HALO, G), F32).at[:CONV_WIDTH].set(conv_w)
    cur = lambda col: pl.BlockSpec((ts, G), lambda b, i: (b * nb + i, col))
    prev = lambda col: pl.BlockSpec((CONV_HALO, G), lambda b, i: (jnp.maximum((b * nb + i) * hb - 1, 0), col))
    vec = pl.BlockSpec((1, G), lambda b, i: (0, 0))
    return pl.pallas_call(
        functools.partial(_conv_kernel, ts=ts),
        out_shape=jax.ShapeDtypeStruct((B * S, G), BF16),
        grid=(B, nb),
        in_specs=[cur(0), cur(1), prev(0), prev(1), pl.BlockSpec((CONV_HALO, G), lambda b, i: (0, 0)), vec, vec, vec],
        out_specs=pl.BlockSpec((ts, G), lambda b, i: (b * nb + i, 0)),
        scratch_shapes=[pltpu.VMEM((ts + CONV_HALO, G), F32), pltpu.VMEM((ts, G), F32),
                        pltpu.VMEM((SUBLANES - 1, ts + CONV_HALO, LANES), F32)],
        compiler_params=_cparams(("arbitrary", "arbitrary"), 32),
        name="conv_mixer",
    )(p, p, p, p, wpad, conv_b.reshape(1, G), ln_g.reshape(1, G), ln_b.reshape(1, G))


_NT = (((1,), (1,)), ((), ()))
SB_HEADS, DIFF_HEADS, MOBA_HEADS = 4, 2, 4
SB_UNDERFLOW = 110.0
DIFF_TILE = 512
MOBA_GROUP = 4
CAST_BLOCK_BYTES = 8 * MIB
ROWCOPY_CHUNK = 2048
ISSUE_UNROLL = 8
WAIT_GROUP = 64
LOG2E = 1.4426950408889634


def _hs(hh):
    return slice(hh * HEAD_DIM, (hh + 1) * HEAD_DIM)


def _sb_kernel(q_ref, k_ref, v_ref, o_ref, *, t, nh):
    i = pl.program_id(2)
    qs = [q_ref[:, _hs(hh)] for hh in range(nh)]
    scale = HEAD_DIM ** -0.5
    row = lax.broadcasted_iota(I32, (t, t), 0)
    col = lax.broadcasted_iota(I32, (t, t), 1)
    tri = jnp.where(row >= col, 1.0, 0.0).astype(BF16)
    past = col < row

    def block(j, st, diag):
        j0 = pl.multiple_of(j * t, t)
        zs = [lax.dot_general(qs[hh], k_ref[pl.ds(j0, t), _hs(hh)], _NT, preferred_element_type=F32) * scale
              for hh in range(nh)]
        his, los = [], []
        for hh in range(nh):
            z = zs[hh]
            sp = jnp.maximum(z, 0.0) + jnp.log(1.0 + jnp.exp(-jnp.abs(z)))
            u = jnp.where(past, sp, 0.0) if diag else sp
            u_hi = u.astype(BF16)
            his.append(u_hi)
            los.append((u - u_hi.astype(F32)).astype(BF16))
        tls = [jnp.dot(his[hh], tri, preferred_element_type=F32) + jnp.dot(los[hh], tri, preferred_element_type=F32)
               for hh in range(nh)]
        a_s = []
        for hh in range(nh):
            e = zs[hh] - (tls[hh] + st[hh][0])
            if diag:
                e = jnp.where(past, e, -jnp.inf)
            a_s.append(jnp.exp(e).astype(BF16))
        return tuple((st[hh][0] + tls[hh][:, 0:1],
                      st[hh][1] + jnp.dot(a_s[hh], v_ref[pl.ds(j0, t), _hs(hh)], preferred_element_type=F32))
                     for hh in range(nh))

    def live(st):
        lo = st[0][0]
        for hh in range(1, nh):
            lo = jnp.minimum(lo, st[hh][0])
        return (jnp.min(lo) < SB_UNDERFLOW).astype(I32)

    def body(c):
        st = block(i - 1 - c[0], c[2], False)
        return c[0] + 1, live(st), st

    init = tuple((jnp.zeros((t, 1), F32), jnp.zeros((t, HEAD_DIM), F32)) for _ in range(nh))
    st = block(i, init, True)
    _, _, st = lax.while_loop(lambda c: (c[0] < i) & (c[1] > 0), body, (jnp.int32(0), live(st), st))
    for hh in range(nh):
        o_ref[:, _hs(hh)] = st[hh][1].astype(o_ref.dtype)


def _diff_kernel(q_ref, k_ref, v_ref, dl_ref, g_ref, o_ref, *, t, nh, lam_init):
    i = pl.program_id(2)
    scale = DIFF_QK_DIM ** -0.5
    assert math.frexp(scale)[0] == 0.5
    lane = lax.broadcasted_iota(I32, (t, HEAD_DIM), 1)
    qs = []
    for hh in range(nh):
        q = q_ref[:, _hs(hh)] * scale
        zero = jnp.zeros_like(q)
        qs.append((jnp.where(lane < DIFF_QK_DIM, q, zero), jnp.where(lane >= DIFF_QK_DIM, q, zero)))
    row = lax.broadcasted_iota(I32, (t, t), 0)
    col = lax.broadcasted_iota(I32, (t, t), 1)
    causal = col <= row
    dl = dl_ref[...]
    lam = (jnp.exp(jnp.sum(dl[0:1] * dl[1:2], axis=1, keepdims=True))
           - jnp.exp(jnp.sum(dl[2:3] * dl[3:4], axis=1, keepdims=True)) + lam_init)

    def block(j, st, diag):
        j0 = pl.multiple_of(j * t, t)
        ss = [lax.dot_general(qs[hh][mi], k_ref[pl.ds(j0, t), _hs(hh)], _NT, preferred_element_type=F32)
              for hh in range(nh) for mi in range(2)]
        prs, alphas, ls, ms = [], [], [], []
        for ci in range(2 * nh):
            m, l, acc = st[ci]
            s = jnp.where(causal, ss[ci], NEG_BIG) if diag else ss[ci]
            m_new = jnp.maximum(m, jnp.max(s, axis=1, keepdims=True))
            alpha = jnp.exp2((m - m_new) * LOG2E)
            pr = jnp.exp2((s - m_new) * LOG2E)
            ls.append(alpha * l + jnp.sum(pr, axis=1, keepdims=True))
            prs.append(pr.astype(BF16))
            alphas.append(alpha)
            ms.append(m_new)
        new = []
        for ci in range(2 * nh):
            vj = v_ref[pl.ds(j0, t), _hs(ci // 2)]
            acc = alphas[ci] * st[ci][2] + jnp.dot(prs[ci], vj, preferred_element_type=F32)
            new.append((ms[ci], ls[ci], acc))
        return tuple(new)

    init = tuple((jnp.full((t, 1), NEG_BIG, F32), jnp.zeros((t, 1), F32), jnp.zeros((t, HEAD_DIM), F32))
                 for _ in range(2 * nh))
    st = block(i, init, True)
    st = lax.fori_loop(0, i, lambda j, s: block(j, s, False), st)
    for hh in range(nh):
        (_, l0, a0), (_, l1, a1) = st[2 * hh], st[2 * hh + 1]
        o = a0 / l0 - lam * (a1 / l1)
        o_ref[:, _hs(hh)] = (_rms(o, g_ref[...]) * (1.0 - lam_init)).astype(o_ref.dtype)


def _moba_kernel(q_ref, k_ref, v_ref, o_ref, km_hi, km_lo, *, t, nh, nkb, grp):
    own = pl.program_id(2)
    scale = HEAD_DIM ** -0.5
    topk = min(MOBA_TOPK, nkb)

    @pl.when(own == 0)
    def _():
        km_hi[...] = jnp.zeros_like(km_hi)
        km_lo[...] = jnp.zeros_like(km_lo)
        for hh in range(nh):
            for blk in range(nkb):
                km = jnp.mean(k_ref[blk * t:(blk + 1) * t, _hs(hh)].astype(F32), axis=0, keepdims=True)
                hi = km.astype(BF16)
                km_hi[hh, blk:blk + 1, :] = hi
                km_lo[hh, blk:blk + 1, :] = (km - hi.astype(F32)).astype(BF16)

    lane = lax.broadcasted_iota(I32, (t, LANES), 1)
    lane_f = lane.astype(F32)
    row = lax.broadcasted_iota(I32, (t, t), 0)
    col = lax.broadcasted_iota(I32, (t, t), 1)
    qs, sels = [], []
    for hh in range(nh):
        q = q_ref[:, _hs(hh)]
        gate = (lax.dot_general(q, km_hi[hh], _NT, preferred_element_type=F32)
                + lax.dot_general(q, km_lo[hh], _NT, preferred_element_type=F32))
        cand = jnp.where(lane < own, gate, -jnp.inf)
        sel = jnp.zeros(gate.shape, F32)
        for _ in range(topk):
            best = jnp.max(cand, axis=1, keepdims=True)
            first = jnp.min(jnp.where(cand == best, lane_f, float(LANES)), axis=1, keepdims=True)
            hit = (lane_f == first) & (best > -jnp.inf)
            sel = jnp.where(hit, 1.0, sel)
            cand = jnp.where(hit, -jnp.inf, cand)
        qs.append(q)
        sels.append(sel)

    def attend(j, nblk, st, masks):
        j0 = pl.multiple_of(j * t, t)
        c = scale * LOG2E
        ss = [lax.dot_general(qs[hh], k_ref[pl.ds(j0, nblk * t), _hs(hh)], _NT, preferred_element_type=F32)
              for hh in range(nh)]
        prs, alphas, ls, ms = [], [], [], []
        for hh in range(nh):
            m, l, acc = st[hh]
            s = ss[hh]
            s = jnp.concatenate([jnp.where(masks[hh][b], s[:, b * t:(b + 1) * t], NEG_BIG) for b in range(nblk)], axis=1)
            m_new = jnp.maximum(m, jnp.max(s, axis=1, keepdims=True))
            alpha = jnp.exp2((m - m_new) * c)
            pr = jnp.exp2((s - m_new) * c)
            ls.append(alpha * l + jnp.sum(pr, axis=1, keepdims=True))
            prs.append(pr.astype(BF16))
            alphas.append(alpha)
            ms.append(m_new)
        new = []
        for hh in range(nh):
            acc = alphas[hh] * st[hh][2] + jnp.dot(prs[hh], v_ref[pl.ds(j0, nblk * t), _hs(hh)], preferred_element_type=F32)
            new.append((ms[hh], ls[hh], acc))
        return tuple(new)

    init = tuple((jnp.full((t, 1), NEG_BIG, F32), jnp.zeros((t, 1), F32), jnp.zeros((t, HEAD_DIM), F32))
                 for _ in range(nh))
    st = attend(own, 1, init, [[col <= row]] * nh)

    def body(g, st):
        picked = [[jnp.max(jnp.where(lane == g * grp + b, sels[hh], 0.0), axis=1, keepdims=True) > 0.5
                   for b in range(grp)] for hh in range(nh)]
        return attend(g * grp, grp, st, picked)

    st = lax.fori_loop(0, (own + grp - 1) // grp, body, st)
    for hh in range(nh):
        _, l, acc = st[hh]
        o_ref[:, _hs(hh)] = (acc / l).astype(o_ref.dtype)


def _attention(kernel_fn, p, extra, extra_specs, scratch, *, B, S, G, c_q, c_k, c_v, t, nh, name):
    H = G // HEAD_DIM
    nq = S // t
    w = nh * HEAD_DIM
    assert H % nh == 0 and c_q % nh == 0 and c_k % nh == 0 and c_v % nh == 0
    return pl.pallas_call(
        kernel_fn,
        out_shape=jax.ShapeDtypeStruct((B * S, G), BF16),
        grid=(B, H // nh, nq),
        in_specs=[pl.BlockSpec((t, w), lambda b, h, i: (b * nq + i, c_q // nh + h)),
                  pl.BlockSpec((S, w), lambda b, h, i: (b, c_k // nh + h)),
                  pl.BlockSpec((S, w), lambda b, h, i: (b, c_v // nh + h))] + extra_specs,
        out_specs=pl.BlockSpec((t, w), lambda b, h, i: (b * nq + i, h)),
        scratch_shapes=scratch,
        compiler_params=_cparams(("arbitrary", "arbitrary", "arbitrary"), 40),
        name=name,
    )(p, p, p, *extra)


def _moe_plan(top_idx, N, tm):
    E, NK = N_EXPERTS, N * TOP_K
    nblocks = (NK + E * (tm - 1) + tm - 1) // tm
    e_flat = top_idx.reshape(NK)
    order = jnp.argsort(e_flat, stable=True).astype(I32)
    counts = jnp.sum((e_flat[:, None] == jnp.arange(E, dtype=I32)[None, :]).astype(I32), axis=0)
    starts = jnp.cumsum(counts) - counts
    nblk_e = (counts + tm - 1) // tm
    bend = jnp.cumsum(nblk_e)
    bstart = bend - nblk_e
    b = jnp.arange(nblocks, dtype=I32)
    blk_e = jnp.minimum(jnp.sum((bend[None, :] <= b[:, None]).astype(I32), axis=1), E - 1).astype(I32)
    blk_off = (b - bstart[blk_e]) * tm
    blk_nv = jnp.where(b < bend[-1], jnp.clip(counts[blk_e] - blk_off, 0, tm), 0).astype(I32)
    ri = jnp.arange(tm, dtype=I32)
    valid = ri[None, :] < blk_nv[:, None]
    src = jnp.clip(starts[blk_e][:, None] + blk_off[:, None] + ri[None, :], 0, NK - 1)
    flat = order[src]
    row_tok = jnp.where(valid, flat // TOP_K, 0).astype(I32)
    row_slot = jnp.where(valid, (flat % TOP_K) * N + flat // TOP_K, -1).astype(I32)
    return blk_e, blk_nv, row_tok, row_slot


def _rowcopy_kernel(cnt_ref, idx_hbm, src, dst, idx_smem, isem, csem, *, ch, rpt, gather):
    s = pl.program_id(0)
    ns = pl.num_programs(0)
    slot = s % 2

    def idx_copy(step, sl):
        return pltpu.make_async_copy(idx_hbm.at[step], idx_smem.at[sl], isem.at[sl])

    @pl.when(s == 0)
    def _():
        idx_copy(0, 0).start()

    idx_copy(s, slot).wait()

    @pl.when(s + 1 < ns)
    def _():
        idx_copy(s + 1, 1 - slot).start()

    def row_copy(r, t):
        blk = pl.ds(pl.multiple_of(r * rpt, rpt), rpt)
        far = pl.ds(pl.multiple_of(t * rpt, rpt), rpt)
        if gather:
            return pltpu.make_async_copy(src.at[far], dst.at[blk], csem)
        return pltpu.make_async_copy(src.at[blk], dst.at[far], csem)

    def issue(g, c):
        for u in range(ISSUE_UNROLL):
            r = g * ISSUE_UNROLL + u
            t = idx_smem[slot, r]
            if gather:
                row_copy(r, t).start(priority=u % 2)
            else:
                @pl.when(t >= 0)
                def _():
                    row_copy(r, t).start(priority=u % 2)
        return c

    def drain_group(g, c):
        for _ in range(WAIT_GROUP):
            row_copy(0, 0).wait()
        return c

    def drain_one(r, c):
        row_copy(0, 0).wait()
        return c

    lax.fori_loop(0, ch // ISSUE_UNROLL, issue, 0)
    n = cnt_ref[s]
    lax.fori_loop(0, n // WAIT_GROUP, drain_group, 0)
    lax.fori_loop(0, n % WAIT_GROUP, drain_one, 0)


def _rowcopy(idx, src, n_dst_tokens, *, rpt, gather, ch=ROWCOPY_CHUNK):
    ch = math.gcd(ch, idx.shape[0])
    ns = idx.shape[0] // ch
    idx2 = idx.reshape(ns, ch)
    cnt = jnp.sum((idx2 >= 0).astype(I32), axis=1)
    blk = pl.BlockSpec((ch * rpt, LANES), lambda s, cnt: (s, 0))
    hbm = pl.BlockSpec(memory_space=pl.ANY)
    return pl.pallas_call(
        functools.partial(_rowcopy_kernel, ch=ch, rpt=rpt, gather=gather),
        out_shape=jax.ShapeDtypeStruct((n_dst_tokens * rpt, LANES), src.dtype),
        grid_spec=pltpu.PrefetchScalarGridSpec(
            num_scalar_prefetch=1, grid=(ns,),
            in_specs=[hbm, hbm if gather else blk],
            out_specs=blk if gather else hbm,
            scratch_shapes=[pltpu.SMEM((2, ch), I32), pltpu.SemaphoreType.DMA((2,)), pltpu.SemaphoreType.DMA]),
        compiler_params=_cparams(("arbitrary",), 48),
        name="row_gather" if gather else "row_scatter",
    )(cnt, idx2, src)


def _expert_kernel(be_ref, nv_ref, tok_hbm, h_hbm, w1_ref, b1_ref, w2_ref, b2_ref, o_ref, tok_smem, tsem, xbuf, gsem,
                   *, tm, rpt):
    b = pl.program_id(0)
    nb = pl.num_programs(0)
    slot = b % 2
    F = w2_ref.shape[1]
    half = w1_ref.shape[1] // 2

    def tok_copy(blk, sl):
        return pltpu.make_async_copy(tok_hbm.at[blk], tok_smem.at[sl], tsem.at[sl])

    def row_copy(sl, r, t):
        return pltpu.make_async_copy(h_hbm.at[pl.ds(pl.multiple_of(t * rpt, rpt), rpt)],
                                     xbuf.at[sl, pl.ds(pl.multiple_of(r * rpt, rpt), rpt)], gsem.at[sl])

    def start_gather(sl):
        def issue(g, c):
            for u in range(ISSUE_UNROLL):
                r = g * ISSUE_UNROLL + u
                row_copy(sl, r, tok_smem[sl, r]).start(priority=u % 2)
            return c
        lax.fori_loop(0, tm // ISSUE_UNROLL, issue, 0)

    def wait_gather(sl):
        def drain(g, c):
            for _ in range(WAIT_GROUP):
                row_copy(sl, 0, 0).wait()
            return c
        lax.fori_loop(0, tm // WAIT_GROUP, drain, 0)

    @pl.when(b == 0)
    def _():
        tok_copy(0, 0).start()
        tok_copy(0, 0).wait()
        start_gather(0)

        @pl.when(nb > 1)
        def _():
            tok_copy(1, 1).start()

    @pl.when(b + 1 < nb)
    def _():
        tok_copy(b + 1, 1 - slot).wait()
        start_gather(1 - slot)

    wait_gather(slot)

    @pl.when(b + 2 < nb)
    def _():
        tok_copy(b + 2, slot).start()

    @pl.when(nv_ref[b] > 0)
    def _():
        lo, hi = _unpack_halves(_load_token_rows(xbuf.at[slot], tm))
        a = (jnp.dot(lo, w1_ref[0, :half, :], preferred_element_type=F32)
             + jnp.dot(hi, w1_ref[0, half:, :], preferred_element_type=F32) + b1_ref[0])
        g = jnp.minimum(a[:, :F], SWIGLU_LIMIT)
        lin = jnp.clip(a[:, F:], -SWIGLU_LIMIT, SWIGLU_LIMIT)
        act = g * jax.nn.sigmoid(SWIGLU_ALPHA * g) * (lin + 1.0)
        y = jnp.dot(act.astype(BF16), w2_ref[0], preferred_element_type=F32) + b2_ref[0]
        _store_token_rows(o_ref, _pack_halves(y))

    @pl.when(nv_ref[b] == 0)
    def _():
        o_ref[...] = jnp.zeros_like(o_ref)


def _experts(blk_e, blk_nv, row_tok, hpk, w1, b1, w2, b2, *, tm, rpt):
    E, D, F2 = w1.shape
    F = F2 // 2
    nblocks = row_tok.shape[0]
    hbm = pl.BlockSpec(memory_space=pl.ANY)
    return pl.pallas_call(
        functools.partial(_expert_kernel, tm=tm, rpt=rpt),
        out_shape=jax.ShapeDtypeStruct((nblocks * tm * rpt, LANES), U32),
        grid_spec=pltpu.PrefetchScalarGridSpec(
            num_scalar_prefetch=2, grid=(nblocks,),
            in_specs=[hbm, hbm,
                      pl.BlockSpec((1, D, F2), lambda b, be, nv: (be[b], 0, 0)),
                      pl.BlockSpec((1, 1, F2), lambda b, be, nv: (be[b], 0, 0)),
                      pl.BlockSpec((1, F, D), lambda b, be, nv: (be[b], 0, 0)),
                      pl.BlockSpec((1, 1, D), lambda b, be, nv: (be[b], 0, 0))],
            out_specs=pl.BlockSpec((tm * rpt, LANES), lambda b, be, nv: (b, 0)),
            scratch_shapes=[pltpu.SMEM((2, tm), I32), pltpu.SemaphoreType.DMA((2,)),
                            pltpu.VMEM((2, tm * rpt, LANES), U32), pltpu.SemaphoreType.DMA((2,))]),
        compiler_params=_cparams(("arbitrary",), 56),
        name="experts",
    )(blk_e, blk_nv, row_tok, hpk, w1, b1.reshape(E, 1, F2), w2, b2.reshape(E, 1, D))


def _moe(hpk, top_idx, w1, b1, w2, b2, *, N, rpt, tm=256):
    blk_e, blk_nv, row_tok, row_slot = _moe_plan(top_idx, N, tm)
    ys = _experts(blk_e, blk_nv, row_tok, hpk, w1, b1, w2, b2, tm=tm, rpt=rpt)
    return _rowcopy(row_slot.reshape(-1), ys, TOP_K * N, rpt=rpt, gather=False)


def kernel(x, c, w_ada, b_ada, ada_table, norm_pre, norm_post, w_in, w_out, conv_w, conv_b, conv_ln_g, conv_ln_b,
           diff_lambda, diff_norm_g, w_router, b_router, w1, b1, w2, b2):
    B, S, D = x.shape
    depth = w_in.shape[0]
    N = B * S
    G = D // N_GROUPS
    H = G // HEAD_DIM
    E = w_router.shape[2]
    assert E == N_EXPERTS and S % MOBA_BLOCK == 0 and G % HEAD_DIM == 0
    nkb = S // MOBA_BLOCK

    cond = _ada(c, w_ada, b_ada)
    xf = x.reshape(N, D)
    h = None
    for l in range(depth):
        lam_init = 0.8 - 0.6 * math.exp(-0.3 * l)
        tab = ada_table[l]
        if l == 0:
            (h,) = _post(xf, cond, tab, tab, S=S, mode="none", nxt="plain", g_pre=norm_pre[l, 0], i_shift=0, i_scale=1)
        p = _matmul([h], _layer_bf16(w_in, l), BF16, 1024, 512 if w_in.shape[2] % 512 == 0 else G)

        y_conv = _conv_mixer(p, conv_w[l], conv_b[l], conv_ln_g[l], conv_ln_b[l], B=B, S=S, G=G)
        t = MOBA_BLOCK
        nh_sb, nh_diff, nh_moba = (math.gcd(n, H) for n in (SB_HEADS, DIFF_HEADS, MOBA_HEADS))
        td = math.gcd(DIFF_TILE, S)
        y_sb = _attention(functools.partial(_sb_kernel, t=t, nh=nh_sb), p, [], [], [], B=B, S=S, G=G,
                          c_q=2 * H, c_k=3 * H, c_v=4 * H, t=t, nh=nh_sb, name="stick_breaking")
        y_diff = _attention(
            functools.partial(_diff_kernel, t=td, nh=nh_diff, lam_init=lam_init), p,
            [diff_lambda[l], diff_norm_g[l].reshape(1, HEAD_DIM)],
            [pl.BlockSpec((4, DIFF_QK_DIM), lambda b, h_, i: (0, 0)), pl.BlockSpec((1, HEAD_DIM), lambda b, h_, i: (0, 0))],
            [], B=B, S=S, G=G, c_q=5 * H, c_k=6 * H, c_v=7 * H, t=td, nh=nh_diff, name="diff_attention")
        y_moba = _attention(
            functools.partial(_moba_kernel, t=t, nh=nh_moba, nkb=nkb, grp=math.gcd(MOBA_GROUP, nkb)), p, [], [],
            [pltpu.VMEM((nh_moba, LANES, HEAD_DIM), BF16), pltpu.VMEM((nh_moba, LANES, HEAD_DIM), BF16)],
            B=B, S=S, G=G, c_q=8 * H, c_k=9 * H, c_v=10 * H, t=t, nh=nh_moba, name="moba")

        y = _matmul([y_conv, y_sb, y_diff, y_moba], _layer_bf16(w_out, l), BF16, 1024, 512)

        wr = jnp.zeros((D, LANES), F32).at[:, :E].set(w_router[l])
        wr_hi = wr.astype(BF16)
        wr_lo = (wr - wr_hi.astype(F32)).astype(BF16)
        br = jnp.zeros((1, LANES), F32).at[0, :E].set(b_router[l])
        xf, hpk, top_idx, gates = _post(xf, cond, tab, tab, S=S, mode="dense", nxt="router", y=y, g_post=norm_post[l, 0],
                                        g_pre=norm_pre[l, 1], router=(wr_hi, wr_lo, br), i_gate=2, i_shift=3, i_scale=4)

        rpt = D // 2 // LANES
        y4 = _moe(hpk, top_idx[:, :TOP_K], _layer_bf16(w1, l), b1[l], _layer_bf16(w2, l), b2[l], N=N, rpt=rpt)
        y4 = y4.reshape(TOP_K, N * rpt, LANES)
        if l + 1 < depth:
            xf, h = _post(xf, cond, tab, ada_table[l + 1], S=S, mode="moe", nxt="plain", y4=y4, g4=gates,
                          g_post=norm_post[l, 1], g_pre=norm_pre[l + 1, 0], i_gate=5, i_shift=0, i_scale=1)
        else:
            (xf,) = _post(xf, cond, tab, tab, S=S, mode="moe", nxt="none", y4=y4, g4=gates, g_post=norm_post[l, 1], i_gate=5)
    return xf.reshape(B, S, D)
```

```python
import functools
import math

import jax
import jax.numpy as jnp
from jax import lax
from jax.experimental import pallas as pl
from jax.experimental.pallas import tpu as pltpu

F32 = jnp.float32
BF16 = jnp.bfloat16
U32 = jnp.uint32
I32 = jnp.int32

N_GROUPS = 4
HEAD_DIM = 128
DIFF_QK_DIM = HEAD_DIM // 2
CONV_WIDTH = 31
MOBA_BLOCK = 256
MOBA_TOPK = 3
N_EXPERTS = 32
TOP_K = 4
SWIGLU_ALPHA = 1.702
SWIGLU_LIMIT = 7.0
RMS_EPS = 1e-6
LN_EPS = 1e-5
N_MOD = 6
LANES = 128
SUBLANES = 8
CONV_HALO = 32
NEG_BIG = -1e30
MIB = 1024 * 1024


def _cparams(sem, vmem_mib):
    return pltpu.CompilerParams(dimension_semantics=sem, vmem_limit_bytes=vmem_mib * MIB)


def _rms(xf, g):
    return xf * lax.rsqrt(jnp.mean(xf * xf, axis=-1, keepdims=True) + RMS_EPS) * g


def _pack_halves(a):
    k = a.shape[1] // 2
    lo = lax.bitcast_convert_type(a[:, :k].astype(BF16).astype(F32), U32)
    hi = lax.bitcast_convert_type(a[:, k:].astype(BF16).astype(F32), U32)
    return (lo >> 16) | (hi & jnp.uint32(0xFFFF0000))


def _unpack_halves(w):
    lo = lax.bitcast_convert_type(w << 16, F32).astype(BF16)
    hi = lax.bitcast_convert_type(w & jnp.uint32(0xFFFF0000), F32).astype(BF16)
    return lo, hi


def _unpack_halves_f32(w):
    return lax.bitcast_convert_type(w << 16, F32), lax.bitcast_convert_type(w & jnp.uint32(0xFFFF0000), F32)


def _store_token_rows(ref, words):
    m, rpt = words.shape[0], words.shape[1] // LANES
    for c in range(rpt):
        ref[pl.ds(c, m, stride=rpt), :] = words[:, c * LANES:(c + 1) * LANES]


def _load_token_rows(ref, m):
    rpt = ref.shape[0] // m
    return jnp.concatenate([ref[pl.ds(c, m, stride=rpt), :] for c in range(rpt)], axis=1)


def _ada_kernel(c_ref, w_ref, b_ref, o_ref):
    c = c_ref[...]
    a = (c * jax.nn.sigmoid(c)).astype(BF16)
    o_ref[...] = jnp.dot(a, w_ref[...].astype(BF16), preferred_element_type=F32) + b_ref[...]


def _ada(c, w_ada, b_ada):
    B, D = c.shape
    n_out = w_ada.shape[1]
    tn = 512
    rows = 8
    c8 = jnp.zeros((rows, D), F32).at[:B].set(c)
    out = pl.pallas_call(
        _ada_kernel,
        out_shape=jax.ShapeDtypeStruct((rows, n_out), F32),
        grid=(n_out // tn,),
        in_specs=[pl.BlockSpec((rows, D), lambda j: (0, 0)),
                  pl.BlockSpec((D, tn), lambda j: (0, j)),
                  pl.BlockSpec((1, tn), lambda j: (0, j))],
        out_specs=pl.BlockSpec((rows, tn), lambda j: (0, j)),
        compiler_params=_cparams(("arbitrary",), 40),
        name="ada",
    )(c8, w_ada, b_ada.reshape(1, n_out))
    return out[:B].reshape(B, N_MOD, D)


def _router_topk(hf, wr_hi_ref, wr_lo_ref, br_ref):
    h_hi = hf.astype(BF16)
    h_lo = (hf - h_hi.astype(F32)).astype(BF16)
    logits = (jnp.dot(h_hi, wr_hi_ref[...], preferred_element_type=F32)
              + jnp.dot(h_lo, wr_hi_ref[...], preferred_element_type=F32)
              + jnp.dot(h_hi, wr_lo_ref[...], preferred_element_type=F32)) + br_ref[...]
    lane = lax.broadcasted_iota(I32, logits.shape, 1)
    lane_f = lane.astype(F32)
    l = jnp.where(lane < N_EXPERTS, logits, -jnp.inf)
    idx_out = jnp.zeros(logits.shape, F32)
    val_out = jnp.full(logits.shape, -jnp.inf, F32)
    m0 = None
    for k in range(TOP_K):
        m = jnp.max(l, axis=1, keepdims=True)
        am = jnp.min(jnp.where(l == m, lane_f, float(LANES)), axis=1, keepdims=True)
        idx_out = jnp.where(lane == k, am, idx_out)
        val_out = jnp.where(lane == k, m, val_out)
        l = jnp.where(lane_f == am, -jnp.inf, l)
        if k == 0:
            m0 = m
    e = jnp.where(lane < TOP_K, jnp.exp(val_out - m0), 0.0)
    gates = e / jnp.sum(e, axis=1, keepdims=True)
    return idx_out.astype(I32), gates


def _post_kernel(*refs, mode, nxt, i_gate, i_shift, i_scale):
    it = iter(refs)
    x_ref = next(it)
    if mode == "dense":
        y_ref = next(it)
    elif mode == "moe":
        y4_ref = next(it)
        g4_ref = next(it)
    cond_ref = next(it)
    tab_ref = next(it)
    tabn_ref = next(it)
    gpost_ref = next(it) if mode != "none" else None
    gpre_ref = next(it) if nxt != "none" else None
    if nxt == "router":
        wr_hi_ref, wr_lo_ref, br_ref = next(it), next(it), next(it)
    outs = list(it)

    mod = cond_ref[0] + tab_ref[...]
    modn = cond_ref[0] + tabn_ref[...]
    x = x_ref[...]
    if mode == "dense":
        y = y_ref[...].astype(F32)
    elif mode == "moe":
        g4 = g4_ref[...]
        y = None
        for k in range(TOP_K):
            lo, hi = _unpack_halves_f32(_load_token_rows(y4_ref.at[k], x.shape[0]))
            yk = jnp.concatenate([lo, hi], axis=1) * g4[:, k:k + 1]
            y = yk if y is None else y + yk
    o = 0
    if mode != "none":
        x = x + mod[i_gate:i_gate + 1, :] * _rms(y, gpost_ref[...])
        outs[o][...] = x
        o += 1
    if nxt != "none":
        hf = _rms(x, gpre_ref[...]) * (1.0 + modn[i_scale:i_scale + 1, :]) + modn[i_shift:i_shift + 1, :]
        if nxt == "plain":
            outs[o][...] = hf.astype(BF16)
        else:
            _store_token_rows(outs[o], _pack_halves(hf))
            idx, gates = _router_topk(hf, wr_hi_ref, wr_lo_ref, br_ref)
            outs[o + 1][...] = idx
            outs[o + 2][...] = gates


def _post(x, cond, tab, tab_next, *, S, mode, nxt, y=None, y4=None, g4=None, g_post=None, g_pre=None,
          router=None, i_gate=0, i_shift=0, i_scale=0, tm=256):
    N, D = x.shape
    nb = S // tm
    rpt = D // 2 // LANES
    row = lambda i: (i, 0)
    args, specs = [x], [pl.BlockSpec((tm, D), row)]
    if mode == "dense":
        args.append(y)
        specs.append(pl.BlockSpec((tm, D), row))
    elif mode == "moe":
        args += [y4, g4]
        specs += [pl.BlockSpec((TOP_K, tm * rpt, LANES), lambda i: (0, i, 0)), pl.BlockSpec((tm, LANES), row)]
    args += [cond, tab, tab_next]
    specs += [pl.BlockSpec((1, N_MOD, D), lambda i: (i // nb, 0, 0)), pl.BlockSpec((N_MOD, D), lambda i: (0, 0)),
              pl.BlockSpec((N_MOD, D), lambda i: (0, 0))]
    if mode != "none":
        args.append(g_post.reshape(1, D))
        specs.append(pl.BlockSpec((1, D), lambda i: (0, 0)))
    if nxt != "none":
        args.append(g_pre.reshape(1, D))
        specs.append(pl.BlockSpec((1, D), lambda i: (0, 0)))
    if nxt == "router":
        args += list(router)
        specs += [pl.BlockSpec((D, LANES), lambda i: (0, 0)), pl.BlockSpec((D, LANES), lambda i: (0, 0)),
                  pl.BlockSpec((1, LANES), lambda i: (0, 0))]
    out_shape, out_specs = [], []
    if mode != "none":
        out_shape.append(jax.ShapeDtypeStruct((N, D), F32))
        out_specs.append(pl.BlockSpec((tm, D), row))
    if nxt == "plain":
        out_shape.append(jax.ShapeDtypeStruct((N, D), BF16))
        out_specs.append(pl.BlockSpec((tm, D), row))
    elif nxt == "router":
        out_shape += [jax.ShapeDtypeStruct((N * rpt, LANES), U32), jax.ShapeDtypeStruct((N, LANES), I32),
                      jax.ShapeDtypeStruct((N, LANES), F32)]
        out_specs += [pl.BlockSpec((tm * rpt, LANES), row), pl.BlockSpec((tm, LANES), row), pl.BlockSpec((tm, LANES), row)]
    return pl.pallas_call(
        functools.partial(_post_kernel, mode=mode, nxt=nxt, i_gate=i_gate, i_shift=i_shift, i_scale=i_scale),
        out_shape=out_shape, grid=(N // tm,), in_specs=specs, out_specs=out_specs,
        compiler_params=_cparams(("arbitrary",), 56),
        name=f"post_{mode}_{nxt}",
    )(*args)


def _mm_kernel(*refs, n_a):
    a_refs, w_ref, o_ref = refs[:n_a], refs[n_a], refs[n_a + 1]
    acc, off = None, 0
    for a in a_refs:
        kg = a.shape[1]
        part = jnp.dot(a[...], w_ref[off:off + kg, :], preferred_element_type=F32)
        acc = part if acc is None else acc + part
        off += kg
    o_ref[...] = acc.astype(o_ref.dtype)


def _matmul(a_list, w, out_dtype, tm, tn):
    N = a_list[0].shape[0]
    K, n_out = w.shape
    specs = [pl.BlockSpec((tm, a.shape[1]), lambda i, j: (i, 0)) for a in a_list]
    specs.append(pl.BlockSpec((K, tn), lambda i, j: (0, j)))
    return pl.pallas_call(
        functools.partial(_mm_kernel, n_a=len(a_list)),
        out_shape=jax.ShapeDtypeStruct((N, n_out), out_dtype),
        grid=(N // tm, n_out // tn), in_specs=specs,
        out_specs=pl.BlockSpec((tm, tn), lambda i, j: (i, j)),
        compiler_params=_cparams(("arbitrary", "arbitrary"), 56),
        name="matmul",
    )(*a_list, w)


def _cast_kernel(x_ref, o_ref):
    o_ref[...] = x_ref[0].astype(o_ref.dtype)


def _layer_bf16(w, l):
    C = w.shape[-1]
    R = math.prod(w.shape[1:-1])
    tr = math.gcd(R, max(8, pl.next_power_of_2(CAST_BLOCK_BYTES // (4 * C) + 1) // 2))
    out = pl.pallas_call(
        _cast_kernel,
        out_shape=jax.ShapeDtypeStruct((R, C), BF16),
        grid=(R // tr,),
        in_specs=[pl.BlockSpec((1, tr, C), lambda i: (l, i, 0))],
        out_specs=pl.BlockSpec((tr, C), lambda i: (i, 0)),
        compiler_params=_cparams(("arbitrary",), 40),
        name="cast_bf16",
    )(w.reshape(w.shape[0], R, C))
    return out.reshape(w.shape[1:])


def _conv_kernel(val_ref, gate_ref, pval_ref, pgate_ref, w_ref, b_ref, lg_ref, lb_ref, o_ref, ubuf, ybuf, shbuf, *, ts):
    G = o_ref.shape[1]
    i = pl.program_id(1)
    v = val_ref[...].astype(F32)
    g = gate_ref[...].astype(F32)
    ubuf[CONV_HALO:CONV_HALO + ts, :] = v * jax.nn.sigmoid(g)
    pv = pval_ref[...].astype(F32)
    pg = pgate_ref[...].astype(F32)
    ubuf[0:CONV_HALO, :] = jnp.where(i > 0, pv * jax.nn.sigmoid(pg), 0.0)
    base = CONV_HALO - (CONV_WIDTH - 1)
    rs = 128
    span = ts + CONV_HALO - SUBLANES

    def chan(c, carry):
        c0 = pl.multiple_of(c * LANES, LANES)
        win = ubuf[:, pl.ds(c0, LANES)]
        for s in range(1, SUBLANES):
            shbuf[s - 1, 0:span, :] = win[s:s + span]
        for r0 in range(0, ts, rs):
            acc = jnp.zeros((rs, LANES), F32)
            for j in range(CONV_WIDTH):
                q, s = divmod(base + j, SUBLANES)
                a0 = r0 + SUBLANES * q
                src = ubuf[a0:a0 + rs, pl.ds(c0, LANES)] if s == 0 else shbuf[s - 1, a0:a0 + rs, :]
                acc = acc + w_ref[j:j + 1, pl.ds(c0, LANES)] * src
            ybuf[r0:r0 + rs, pl.ds(c0, LANES)] = acc
        return carry

    lax.fori_loop(0, G // LANES, chan, 0)
    for r0 in range(0, ts, 64):
        y = ybuf[r0:r0 + 64, :] + b_ref[...]
        mu = jnp.mean(y, axis=-1, keepdims=True)
        d = y - mu
        var = jnp.mean(d * d, axis=-1, keepdims=True)
        z = d * lax.rsqrt(var + LN_EPS) * lg_ref[...] + lb_ref[...]
        o_ref[r0:r0 + 64, :] = (z * jax.nn.sigmoid(z)).astype(o_ref.dtype)


def _conv_mixer(p, conv_w, conv_b, ln_g, ln_b, *, B, S, G, ts=256):
    nb = S // ts
    hb = ts // CONV_HALO
    wpad = jnp.zeros((CONV_HALO, G), F32).at[:CONV_WIDTH].set(conv_w)
    cur = lambda col: pl.BlockSpec((ts, G), lambda b, i: (b * nb + i, col))
    prev = lambda col: pl.BlockSpec((CONV_HALO, G), lambda b, i: (jnp.maximum((b * nb + i) * hb - 1, 0), col))
    vec = pl.BlockSpec((1, G), lambda b, i: (0, 0))
    return pl.pallas_call(
        functools.partial(_conv_kernel, ts=ts),
        out_shape=jax.ShapeDtypeStruct((B * S, G), BF16),
        grid=(B, nb),
        in_specs=[cur(0), cur(1), prev(0), prev(1), pl.BlockSpec((CONV_HALO, G), lambda b, i: (0, 0)), vec, vec, vec],
        out_specs=pl.BlockSpec((ts, G), lambda b, i: (b * nb + i, 0)),
        scratch_shapes=[pltpu.VMEM((ts + CONV_HALO, G), F32), pltpu.VMEM((ts, G), F32),
                        pltpu.VMEM((SUBLANES - 1, ts + CONV_HALO, LANES), F32)],
        compiler_params=_cparams(("arbitrary", "arbitrary"), 32),
        name="conv_mixer",
    )(p, p, p, p, wpad, conv_b.reshape(1, G), ln_g.reshape(1, G), ln_b.reshape(1, G))


_NT = (((1,), (1,)), ((), ()))
SB_HEADS, DIFF_HEADS, MOBA_HEADS = 4, 2, 4
SB_UNDERFLOW = 110.0
DIFF_TILE = 512
MOBA_GROUP = 4
CAST_BLOCK_BYTES = 8 * MIB
ROWCOPY_CHUNK = 2048
ISSUE_UNROLL = 8
WAIT_GROUP = 64
LOG2E = 1.4426950408889634


def _hs(hh):
    return slice(hh * HEAD_DIM, (hh + 1) * HEAD_DIM)


def _sb_kernel(q_ref, k_ref, v_ref, o_ref, *, t, nh):
    i = pl.program_id(2)
    qs = [q_ref[:, _hs(hh)] for hh in range(nh)]
    scale = HEAD_DIM ** -0.5
    row = lax.broadcasted_iota(I32, (t, t), 0)
    col = lax.broadcasted_iota(I32, (t, t), 1)
    tri = jnp.where(row >= col, 1.0, 0.0).astype(BF16)
    past = col < row

    def block(j, st, diag):
        j0 = pl.multiple_of(j * t, t)
        zs = [lax.dot_general(qs[hh], k_ref[pl.ds(j0, t), _hs(hh)], _NT, preferred_element_type=F32) * scale
              for hh in range(nh)]
        his, los = [], []
        for hh in range(nh):
            z = zs[hh]
            sp = jnp.maximum(z, 0.0) + jnp.log(1.0 + jnp.exp(-jnp.abs(z)))
            u = jnp.where(past, sp, 0.0) if diag else sp
            u_hi = u.astype(BF16)
            his.append(u_hi)
            los.append((u - u_hi.astype(F32)).astype(BF16))
        tls = [jnp.dot(his[hh], tri, preferred_element_type=F32) + jnp.dot(los[hh], tri, preferred_element_type=F32)
               for hh in range(nh)]
        a_s = []
        for hh in range(nh):
            e = zs[hh] - (tls[hh] + st[hh][0])
            if diag:
                e = jnp.where(past, e, -jnp.inf)
            a_s.append(jnp.exp(e).astype(BF16))
        return tuple((st[hh][0] + tls[hh][:, 0:1],
                      st[hh][1] + jnp.dot(a_s[hh], v_ref[pl.ds(j0, t), _hs(hh)], preferred_element_type=F32))
                     for hh in range(nh))

    def live(st):
        lo = st[0][0]
        for hh in range(1, nh):
            lo = jnp.minimum(lo, st[hh][0])
        return (jnp.min(lo) < SB_UNDERFLOW).astype(I32)

    def body(c):
        st = block(i - 1 - c[0], c[2], False)
        return c[0] + 1, live(st), st

    init = tuple((jnp.zeros((t, 1), F32), jnp.zeros((t, HEAD_DIM), F32)) for _ in range(nh))
    st = block(i, init, True)
    _, _, st = lax.while_loop(lambda c: (c[0] < i) & (c[1] > 0), body, (jnp.int32(0), live(st), st))
    for hh in range(nh):
        o_ref[:, _hs(hh)] = st[hh][1].astype(o_ref.dtype)


def _diff_kernel(q_ref, k_ref, v_ref, dl_ref, g_ref, o_ref, *, t, nh, lam_init):
    i = pl.program_id(2)
    scale = DIFF_QK_DIM ** -0.5
    assert math.frexp(scale)[0] == 0.5
    lane = lax.broadcasted_iota(I32, (t, HEAD_DIM), 1)
    qs = []
    for hh in range(nh):
        q = q_ref[:, _hs(hh)] * scale
        zero = jnp.zeros_like(q)
        qs.append((jnp.where(lane < DIFF_QK_DIM, q, zero), jnp.where(lane >= DIFF_QK_DIM, q, zero)))
    row = lax.broadcasted_iota(I32, (t, t), 0)
    col = lax.broadcasted_iota(I32, (t, t), 1)
    causal = col <= row
    dl = dl_ref[...]
    lam = (jnp.exp(jnp.sum(dl[0:1] * dl[1:2], axis=1, keepdims=True))
           - jnp.exp(jnp.sum(dl[2:3] * dl[3:4], axis=1, keepdims=True)) + lam_init)

    def block(j, st, diag):
        j0 = pl.multiple_of(j * t, t)
        ss = [lax.dot_general(qs[hh][mi], k_ref[pl.ds(j0, t), _hs(hh)], _NT, preferred_element_type=F32)
              for hh in range(nh) for mi in range(2)]
        prs, alphas, ls, ms = [], [], [], []
        for ci in range(2 * nh):
            m, l, acc = st[ci]
            s = jnp.where(causal, ss[ci], NEG_BIG) if diag else ss[ci]
            m_new = jnp.maximum(m, jnp.max(s, axis=1, keepdims=True))
            alpha = jnp.exp2((m - m_new) * LOG2E)
            pr = jnp.exp2((s - m_new) * LOG2E)
            ls.append(alpha * l + jnp.sum(pr, axis=1, keepdims=True))
            prs.append(pr.astype(BF16))
            alphas.append(alpha)
            ms.append(m_new)
        new = []
        for ci in range(2 * nh):
            vj = v_ref[pl.ds(j0, t), _hs(ci // 2)]
            acc = alphas[ci] * st[ci][2] + jnp.dot(prs[ci], vj, preferred_element_type=F32)
            new.append((ms[ci], ls[ci], acc))
        return tuple(new)

    init = tuple((jnp.full((t, 1), NEG_BIG, F32), jnp.zeros((t, 1), F32), jnp.zeros((t, HEAD_DIM), F32))
                 for _ in range(2 * nh))
    st = block(i, init, True)
    st = lax.fori_loop(0, i, lambda j, s: block(j, s, False), st)
    for hh in range(nh):
        (_, l0, a0), (_, l1, a1) = st[2 * hh], st[2 * hh + 1]
        o = a0 / l0 - lam * (a1 / l1)
        o_ref[:, _hs(hh)] = (_rms(o, g_ref[...]) * (1.0 - lam_init)).astype(o_ref.dtype)


def _moba_kernel(q_ref, k_ref, v_ref, o_ref, km_hi, km_lo, *, t, nh, nkb, grp):
    own = pl.program_id(2)
    scale = HEAD_DIM ** -0.5
    topk = min(MOBA_TOPK, nkb)

    @pl.when(own == 0)
    def _():
        km_hi[...] = jnp.zeros_like(km_hi)
        km_lo[...] = jnp.zeros_like(km_lo)
        for hh in range(nh):
            for blk in range(nkb):
                km = jnp.mean(k_ref[blk * t:(blk + 1) * t, _hs(hh)].astype(F32), axis=0, keepdims=True)
                hi = km.astype(BF16)
                km_hi[hh, blk:blk + 1, :] = hi
                km_lo[hh, blk:blk + 1, :] = (km - hi.astype(F32)).astype(BF16)

    lane = lax.broadcasted_iota(I32, (t, LANES), 1)
    lane_f = lane.astype(F32)
    row = lax.broadcasted_iota(I32, (t, t), 0)
    col = lax.broadcasted_iota(I32, (t, t), 1)
    qs, sels = [], []
    for hh in range(nh):
        q = q_ref[:, _hs(hh)]
        gate = (lax.dot_general(q, km_hi[hh], _NT, preferred_element_type=F32)
                + lax.dot_general(q, km_lo[hh], _NT, preferred_element_type=F32))
        cand = jnp.where(lane < own, gate, -jnp.inf)
        sel = jnp.zeros(gate.shape, F32)
        for _ in range(topk):
            best = jnp.max(cand, axis=1, keepdims=True)
            first = jnp.min(jnp.where(cand == best, lane_f, float(LANES)), axis=1, keepdims=True)
            hit = (lane_f == first) & (best > -jnp.inf)
            sel = jnp.where(hit, 1.0, sel)
            cand = jnp.where(hit, -jnp.inf, cand)
        qs.append(q)
        sels.append(sel)

    def attend(j, nblk, st, masks):
        j0 = pl.multiple_of(j * t, t)
        c = scale * LOG2E
        ss = [lax.dot_general(qs[hh], k_ref[pl.ds(j0, nblk * t), _hs(hh)], _NT, preferred_element_type=F32)
              for hh in range(nh)]
        prs, alphas, ls, ms = [], [], [], []
        for hh in range(nh):
            m, l, acc = st[hh]
            s = ss[hh]
            s = jnp.concatenate([jnp.where(masks[hh][b], s[:, b * t:(b + 1) * t], NEG_BIG) for b in range(nblk)], axis=1)
            m_new = jnp.maximum(m, jnp.max(s, axis=1, keepdims=True))
            alpha = jnp.exp2((m - m_new) * c)
            pr = jnp.exp2((s - m_new) * c)
            ls.append(alpha * l + jnp.sum(pr, axis=1, keepdims=True))
            prs.append(pr.astype(BF16))
            alphas.append(alpha)
            ms.append(m_new)
        new = []
        for hh in range(nh):
            acc = alphas[hh] * st[hh][2] + jnp.dot(prs[hh], v_ref[pl.ds(j0, nblk * t), _hs(hh)], preferred_element_type=F32)
            new.append((ms[hh], ls[hh], acc))
        return tuple(new)

    init = tuple((jnp.full((t, 1), NEG_BIG, F32), jnp.zeros((t, 1), F32), jnp.zeros((t, HEAD_DIM), F32))
                 for _ in range(nh))
    st = attend(own, 1, init, [[col <= row]] * nh)

    def body(g, st):
        picked = [[jnp.max(jnp.where(lane == g * grp + b, sels[hh], 0.0), axis=1, keepdims=True) > 0.5
                   for b in range(grp)] for hh in range(nh)]
        return attend(g * grp, grp, st, picked)

    st = lax.fori_loop(0, (own + grp - 1) // grp, body, st)
    for hh in range(nh):
        _, l, acc = st[hh]
        o_ref[:, _hs(hh)] = (acc / l).astype(o_ref.dtype)


def _attention(kernel_fn, p, extra, extra_specs, scratch, *, B, S, G, c_q, c_k, c_v, t, nh, name):
    H = G // HEAD_DIM
    nq = S // t
    w = nh * HEAD_DIM
    assert H % nh == 0 and c_q % nh == 0 and c_k % nh == 0 and c_v % nh == 0
    return pl.pallas_call(
        kernel_fn,
        out_shape=jax.ShapeDtypeStruct((B * S, G), BF16),
        grid=(B, H // nh, nq),
        in_specs=[pl.BlockSpec((t, w), lambda b, h, i: (b * nq + i, c_q // nh + h)),
                  pl.BlockSpec((S, w), lambda b, h, i: (b, c_k // nh + h)),
                  pl.BlockSpec((S, w), lambda b, h, i: (b, c_v // nh + h))] + extra_specs,
        out_specs=pl.BlockSpec((t, w), lambda b, h, i: (b * nq + i, h)),
        scratch_shapes=scratch,
        compiler_params=_cparams(("arbitrary", "arbitrary", "arbitrary"), 40),
        name=name,
    )(p, p, p, *extra)


def _moe_plan(top_idx, N, tm):
    E, NK = N_EXPERTS, N * TOP_K
    nblocks = (NK + E * (tm - 1) + tm - 1) // tm
    e_flat = top_idx.reshape(NK)
    order = jnp.argsort(e_flat, stable=True).astype(I32)
    counts = jnp.sum((e_flat[:, None] == jnp.arange(E, dtype=I32)[None, :]).astype(I32), axis=0)
    starts = jnp.cumsum(counts) - counts
    nblk_e = (counts + tm - 1) // tm
    bend = jnp.cumsum(nblk_e)
    bstart = bend - nblk_e
    b = jnp.arange(nblocks, dtype=I32)
    blk_e = jnp.minimum(jnp.sum((bend[None, :] <= b[:, None]).astype(I32), axis=1), E - 1).astype(I32)
    blk_off = (b - bstart[blk_e]) * tm
    blk_nv = jnp.where(b < bend[-1], jnp.clip(counts[blk_e] - blk_off, 0, tm), 0).astype(I32)
    ri = jnp.arange(tm, dtype=I32)
    valid = ri[None, :] < blk_nv[:, None]
    src = jnp.clip(starts[blk_e][:, None] + blk_off[:, None] + ri[None, :], 0, NK - 1)
    flat = order[src]
    row_tok = jnp.where(valid, flat // TOP_K, 0).astype(I32)
    row_slot = jnp.where(valid, (flat % TOP_K) * N + flat // TOP_K, -1).astype(I32)
    return blk_e, blk_nv, row_tok, row_slot


def _rowcopy_kernel(cnt_ref, idx_hbm, src, dst, idx_smem, isem, csem, *, ch, rpt, gather):
    s = pl.program_id(0)
    ns = pl.num_programs(0)
    slot = s % 2

    def idx_copy(step, sl):
        return pltpu.make_async_copy(idx_hbm.at[step], idx_smem.at[sl], isem.at[sl])

    @pl.when(s == 0)
    def _():
        idx_copy(0, 0).start()

    idx_copy(s, slot).wait()

    @pl.when(s + 1 < ns)
    def _():
        idx_copy(s + 1, 1 - slot).start()

    def row_copy(r, t):
        blk = pl.ds(pl.multiple_of(r * rpt, rpt), rpt)
        far = pl.ds(pl.multiple_of(t * rpt, rpt), rpt)
        if gather:
            return pltpu.make_async_copy(src.at[far], dst.at[blk], csem)
        return pltpu.make_async_copy(src.at[blk], dst.at[far], csem)

    def issue(g, c):
        for u in range(ISSUE_UNROLL):
            r = g * ISSUE_UNROLL + u
            t = idx_smem[slot, r]
            if gather:
                row_copy(r, t).start(priority=u % 2)
            else:
                @pl.when(t >= 0)
                def _():
                    row_copy(r, t).start(priority=u % 2)
        return c

    def drain_group(g, c):
        for _ in range(WAIT_GROUP):
            row_copy(0, 0).wait()
        return c

    def drain_one(r, c):
        row_copy(0, 0).wait()
        return c

    lax.fori_loop(0, ch // ISSUE_UNROLL, issue, 0)
    n = cnt_ref[s]
    lax.fori_loop(0, n // WAIT_GROUP, drain_group, 0)
    lax.fori_loop(0, n % WAIT_GROUP, drain_one, 0)


def _rowcopy(idx, src, n_dst_tokens, *, rpt, gather, ch=ROWCOPY_CHUNK):
    ch = math.gcd(ch, idx.shape[0])
    ns = idx.shape[0] // ch
    idx2 = idx.reshape(ns, ch)
    cnt = jnp.sum((idx2 >= 0).astype(I32), axis=1)
    blk = pl.BlockSpec((ch * rpt, LANES), lambda s, cnt: (s, 0))
    hbm = pl.BlockSpec(memory_space=pl.ANY)
    return pl.pallas_call(
        functools.partial(_rowcopy_kernel, ch=ch, rpt=rpt, gather=gather),
        out_shape=jax.ShapeDtypeStruct((n_dst_tokens * rpt, LANES), src.dtype),
        grid_spec=pltpu.PrefetchScalarGridSpec(
            num_scalar_prefetch=1, grid=(ns,),
            in_specs=[hbm, hbm if gather else blk],
            out_specs=blk if gather else hbm,
            scratch_shapes=[pltpu.SMEM((2, ch), I32), pltpu.SemaphoreType.DMA((2,)), pltpu.SemaphoreType.DMA]),
        compiler_params=_cparams(("arbitrary",), 48),
        name="row_gather" if gather else "row_scatter",
    )(cnt, idx2, src)


def _expert_kernel(be_ref, nv_ref, tok_hbm, h_hbm, w1_ref, b1_ref, w2_ref, b2_ref, o_ref, tok_smem, tsem, xbuf, gsem,
                   *, tm, rpt):
    b = pl.program_id(0)
    nb = pl.num_programs(0)
    slot = b % 2
    F = w2_ref.shape[1]
    half = w1_ref.shape[1] // 2

    def tok_copy(blk, sl):
        return pltpu.make_async_copy(tok_hbm.at[blk], tok_smem.at[sl], tsem.at[sl])

    def row_copy(sl, r, t):
        return pltpu.make_async_copy(h_hbm.at[pl.ds(pl.multiple_of(t * rpt, rpt), rpt)],
                                     xbuf.at[sl, pl.ds(pl.multiple_of(r * rpt, rpt), rpt)], gsem.at[sl])

    def start_gather(sl, inline=False):
        if inline:
            for r in range(tm):
                row_copy(sl, r, tok_smem[sl, r]).start(priority=r % 2)
            return

        def issue(g, c):
            for u in range(ISSUE_UNROLL):
                r = g * ISSUE_UNROLL + u
                row_copy(sl, r, tok_smem[sl, r]).start(priority=u % 2)
            return c
        lax.fori_loop(0, tm // ISSUE_UNROLL, issue, 0)

    def wait_gather(sl):
        def drain(g, c):
            for _ in range(WAIT_GROUP):
                row_copy(sl, 0, 0).wait()
            return c
        lax.fori_loop(0, tm // WAIT_GROUP, drain, 0)

    @pl.when(b == 0)
    def _():
        tok_copy(0, 0).start()
        tok_copy(0, 0).wait()
        start_gather(0)
        tok_copy(1, 1).start()

    @pl.when(b + 1 < nb)
    def _():
        tok_copy(b + 1, 1 - slot).wait()

    wait_gather(slot)

    @pl.when(b + 2 < nb)
    def _():
        tok_copy(b + 2, slot).start()

    @pl.when(nv_ref[b] > 0)
    def _():
        start_gather(1 - slot, inline=True)
        lo, hi = _unpack_halves(_load_token_rows(xbuf.at[slot], tm))
        a = (jnp.dot(lo, w1_ref[0, :half, :], preferred_element_type=F32)
             + jnp.dot(hi, w1_ref[0, half:, :], preferred_element_type=F32) + b1_ref[0])
        g = jnp.minimum(a[:, :F], SWIGLU_LIMIT)
        lin = jnp.clip(a[:, F:], -SWIGLU_LIMIT, SWIGLU_LIMIT)
        act = g * jax.nn.sigmoid(SWIGLU_ALPHA * g) * (lin + 1.0)
        y = jnp.dot(act.astype(BF16), w2_ref[0], preferred_element_type=F32) + b2_ref[0]
        _store_token_rows(o_ref, _pack_halves(y))

    @pl.when(nv_ref[b] == 0)
    def _():
        start_gather(1 - slot)
        o_ref[...] = jnp.zeros_like(o_ref)

    @pl.when(b == nb - 1)
    def _():
        wait_gather(1 - slot)


def _experts(blk_e, blk_nv, row_tok, hpk, w1, b1, w2, b2, *, tm, rpt):
    E, D, F2 = w1.shape
    F = F2 // 2
    nblocks = row_tok.shape[0]
    assert nblocks >= 2
    hbm = pl.BlockSpec(memory_space=pl.ANY)
    return pl.pallas_call(
        functools.partial(_expert_kernel, tm=tm, rpt=rpt),
        out_shape=jax.ShapeDtypeStruct((nblocks * tm * rpt, LANES), U32),
        grid_spec=pltpu.PrefetchScalarGridSpec(
            num_scalar_prefetch=2, grid=(nblocks,),
            in_specs=[hbm, hbm,
                      pl.BlockSpec((1, D, F2), lambda b, be, nv: (be[b], 0, 0)),
                      pl.BlockSpec((1, 1, F2), lambda b, be, nv: (be[b], 0, 0)),
                      pl.BlockSpec((1, F, D), lambda b, be, nv: (be[b], 0, 0)),
                      pl.BlockSpec((1, 1, D), lambda b, be, nv: (be[b], 0, 0))],
            out_specs=pl.BlockSpec((tm * rpt, LANES), lambda b, be, nv: (b, 0)),
            scratch_shapes=[pltpu.SMEM((2, tm), I32), pltpu.SemaphoreType.DMA((2,)),
                            pltpu.VMEM((2, tm * rpt, LANES), U32), pltpu.SemaphoreType.DMA((2,))]),
        compiler_params=_cparams(("arbitrary",), 56),
        name="experts",
    )(blk_e, blk_nv, row_tok, hpk, w1, b1.reshape(E, 1, F2), w2, b2.reshape(E, 1, D))


def _moe(hpk, top_idx, w1, b1, w2, b2, *, N, rpt, tm=256):
    blk_e, blk_nv, row_tok, row_slot = _moe_plan(top_idx, N, tm)
    ys = _experts(blk_e, blk_nv, row_tok, hpk, w1, b1, w2, b2, tm=tm, rpt=rpt)
    return _rowcopy(row_slot.reshape(-1), ys, TOP_K * N, rpt=rpt, gather=False)


def kernel(x, c, w_ada, b_ada, ada_table, norm_pre, norm_post, w_in, w_out, conv_w, conv_b, conv_ln_g, conv_ln_b,
           diff_lambda, diff_norm_g, w_router, b_router, w1, b1, w2, b2):
    B, S, D = x.shape
    depth = w_in.shape[0]
    N = B * S
    G = D // N_GROUPS
    H = G // HEAD_DIM
    E = w_router.shape[2]
    assert E == N_EXPERTS and S % MOBA_BLOCK == 0 and G % HEAD_DIM == 0
    nkb = S // MOBA_BLOCK

    cond = _ada(c, w_ada, b_ada)
    xf = x.reshape(N, D)
    h = None
    for l in range(depth):
        lam_init = 0.8 - 0.6 * math.exp(-0.3 * l)
        tab = ada_table[l]
        if l == 0:
            (h,) = _post(xf, cond, tab, tab, S=S, mode="none", nxt="plain", g_pre=norm_pre[l, 0], i_shift=0, i_scale=1)
        p = _matmul([h], _layer_bf16(w_in, l), BF16, 1024, 512 if w_in.shape[2] % 512 == 0 else G)

        y_conv = _conv_mixer(p, conv_w[l], conv_b[l], conv_ln_g[l], conv_ln_b[l], B=B, S=S, G=G)
        t = MOBA_BLOCK
        nh_sb, nh_diff, nh_moba = (math.gcd(n, H) for n in (SB_HEADS, DIFF_HEADS, MOBA_HEADS))
        td = math.gcd(DIFF_TILE, S)
        y_sb = _attention(functools.partial(_sb_kernel, t=t, nh=nh_sb), p, [], [], [], B=B, S=S, G=G,
                          c_q=2 * H, c_k=3 * H, c_v=4 * H, t=t, nh=nh_sb, name="stick_breaking")
        y_diff = _attention(
            functools.partial(_diff_kernel, t=td, nh=nh_diff, lam_init=lam_init), p,
            [diff_lambda[l], diff_norm_g[l].reshape(1, HEAD_DIM)],
            [pl.BlockSpec((4, DIFF_QK_DIM), lambda b, h_, i: (0, 0)), pl.BlockSpec((1, HEAD_DIM), lambda b, h_, i: (0, 0))],
            [], B=B, S=S, G=G, c_q=5 * H, c_k=6 * H, c_v=7 * H, t=td, nh=nh_diff, name="diff_attention")
        y_moba = _attention(
            functools.partial(_moba_kernel, t=t, nh=nh_moba, nkb=nkb, grp=math.gcd(MOBA_GROUP, nkb)), p, [], [],
            [pltpu.VMEM((nh_moba, LANES, HEAD_DIM), BF16), pltpu.VMEM((nh_moba, LANES, HEAD_DIM), BF16)],
            B=B, S=S, G=G, c_q=8 * H, c_k=9 * H, c_v=10 * H, t=t, nh=nh_moba, name="moba")

        y = _matmul([y_conv, y_sb, y_diff, y_moba], _layer_bf16(w_out, l), BF16, 1024, 512)

        wr = jnp.zeros((D, LANES), F32).at[:, :E].set(w_router[l])
        wr_hi = wr.astype(BF16)
        wr_lo = (wr - wr_hi.astype(F32)).astype(BF16)
        br = jnp.zeros((1, LANES), F32).at[0, :E].set(b_router[l])
        xf, hpk, top_idx, gates = _post(xf, cond, tab, tab, S=S, mode="dense", nxt="router", y=y, g_post=norm_post[l, 0],
                                        g_pre=norm_pre[l, 1], router=(wr_hi, wr_lo, br), i_gate=2, i_shift=3, i_scale=4)

        rpt = D // 2 // LANES
        y4 = _moe(hpk, top_idx[:, :TOP_K], _layer_bf16(w1, l), b1[l], _layer_bf16(w2, l), b2[l], N=N, rpt=rpt)
        y4 = y4.reshape(TOP_K, N * rpt, LANES)
        if l + 1 < depth:
            xf, h = _post(xf, cond, tab, ada_table[l + 1], S=S, mode="moe", nxt="plain", y4=y4, g4=gates,
                          g_post=norm_post[l, 1], g_pre=norm_pre[l + 1, 0], i_gate=5, i_shift=0, i_scale=1)
        else:
            (xf,) = _post(xf, cond, tab, tab, S=S, mode="moe", nxt="none", y4=y4, g4=gates, g_post=norm_post[l, 1], i_gate=5)
    return xf.reshape(B, S, D)
```

```python
import functools
import math

import jax
import jax.numpy as jnp
from jax import lax
from jax.experimental import pallas as pl
from jax.experimental.pallas import tpu as pltpu

F32 = jnp.float32
BF16 = jnp.bfloat16
U32 = jnp.uint32
I32 = jnp.int32

N_GROUPS = 4
HEAD_DIM = 128
DIFF_QK_DIM = HEAD_DIM // 2
CONV_WIDTH = 31
MOBA_BLOCK = 256
MOBA_TOPK = 3
N_EXPERTS = 32
TOP_K = 4
SWIGLU_ALPHA = 1.702
SWIGLU_LIMIT = 7.0
RMS_EPS = 1e-6
LN_EPS = 1e-5
N_MOD = 6
LANES = 128
SUBLANES = 8
CONV_HALO = 32
NEG_BIG = -1e30
MIB = 1024 * 1024


def _cparams(sem, vmem_mib):
    return pltpu.CompilerParams(dimension_semantics=sem, vmem_limit_bytes=vmem_mib * MIB)


def _rms(xf, g):
    return xf * lax.rsqrt(jnp.mean(xf * xf, axis=-1, keepdims=True) + RMS_EPS) * g


def _pack_halves(a):
    k = a.shape[1] // 2
    lo = lax.bitcast_convert_type(a[:, :k].astype(BF16).astype(F32), U32)
    hi = lax.bitcast_convert_type(a[:, k:].astype(BF16).astype(F32), U32)
    return (lo >> 16) | (hi & jnp.uint32(0xFFFF0000))


def _unpack_halves(w):
    lo = lax.bitcast_convert_type(w << 16, F32).astype(BF16)
    hi = lax.bitcast_convert_type(w & jnp.uint32(0xFFFF0000), F32).astype(BF16)
    return lo, hi


def _unpack_halves_f32(w):
    return lax.bitcast_convert_type(w << 16, F32), lax.bitcast_convert_type(w & jnp.uint32(0xFFFF0000), F32)


def _store_token_rows(ref, words):
    m, rpt = words.shape[0], words.shape[1] // LANES
    for c in range(rpt):
        ref[pl.ds(c, m, stride=rpt), :] = words[:, c * LANES:(c + 1) * LANES]


def _load_token_rows(ref, m):
    rpt = ref.shape[0] // m
    return jnp.concatenate([ref[pl.ds(c, m, stride=rpt), :] for c in range(rpt)], axis=1)


def _ada_kernel(c_ref, w_ref, b_ref, o_ref):
    c = c_ref[...]
    a = (c * jax.nn.sigmoid(c)).astype(BF16)
    o_ref[...] = jnp.dot(a, w_ref[...].astype(BF16), preferred_element_type=F32) + b_ref[...]


def _ada(c, w_ada, b_ada):
    B, D = c.shape
    n_out = w_ada.shape[1]
    tn = 512
    rows = 8
    c8 = jnp.zeros((rows, D), F32).at[:B].set(c)
    out = pl.pallas_call(
        _ada_kernel,
        out_shape=jax.ShapeDtypeStruct((rows, n_out), F32),
        grid=(n_out // tn,),
        in_specs=[pl.BlockSpec((rows, D), lambda j: (0, 0)),
                  pl.BlockSpec((D, tn), lambda j: (0, j)),
                  pl.BlockSpec((1, tn), lambda j: (0, j))],
        out_specs=pl.BlockSpec((rows, tn), lambda j: (0, j)),
        compiler_params=_cparams(("arbitrary",), 40),
        name="ada",
    )(c8, w_ada, b_ada.reshape(1, n_out))
    return out[:B].reshape(B, N_MOD, D)


def _router_topk(hf, wr_hi_ref, wr_lo_ref, br_ref):
    h_hi = hf.astype(BF16)
    h_lo = (hf - h_hi.astype(F32)).astype(BF16)
    logits = (jnp.dot(h_hi, wr_hi_ref[...], preferred_element_type=F32)
              + jnp.dot(h_lo, wr_hi_ref[...], preferred_element_type=F32)
              + jnp.dot(h_hi, wr_lo_ref[...], preferred_element_type=F32)) + br_ref[...]
    lane = lax.broadcasted_iota(I32, logits.shape, 1)
    lane_f = lane.astype(F32)
    l = jnp.where(lane < N_EXPERTS, logits, -jnp.inf)
    idx_out = jnp.zeros(logits.shape, F32)
    val_out = jnp.full(logits.shape, -jnp.inf, F32)
    m0 = None
    for k in range(TOP_K):
        m = jnp.max(l, axis=1, keepdims=True)
        am = jnp.min(jnp.where(l == m, lane_f, float(LANES)), axis=1, keepdims=True)
        idx_out = jnp.where(lane == k, am, idx_out)
        val_out = jnp.where(lane == k, m, val_out)
        l = jnp.where(lane_f == am, -jnp.inf, l)
        if k == 0:
            m0 = m
    e = jnp.where(lane < TOP_K, jnp.exp(val_out - m0), 0.0)
    gates = e / jnp.sum(e, axis=1, keepdims=True)
    return idx_out.astype(I32), gates


def _post_kernel(*refs, mode, nxt, i_gate, i_shift, i_scale):
    it = iter(refs)
    x_ref = next(it)
    if mode == "dense":
        y_ref = next(it)
    elif mode == "moe":
        y4_ref = next(it)
        g4_ref = next(it)
    cond_ref = next(it)
    tab_ref = next(it)
    tabn_ref = next(it)
    gpost_ref = next(it) if mode != "none" else None
    gpre_ref = next(it) if nxt != "none" else None
    if nxt == "router":
        wr_hi_ref, wr_lo_ref, br_ref = next(it), next(it), next(it)
    outs = list(it)

    mod = cond_ref[0] + tab_ref[...]
    modn = cond_ref[0] + tabn_ref[...]
    x = x_ref[...]
    if mode == "dense":
        y = y_ref[...].astype(F32)
    elif mode == "moe":
        g4 = g4_ref[...]
        y = None
        for k in range(TOP_K):
            lo, hi = _unpack_halves_f32(_load_token_rows(y4_ref.at[k], x.shape[0]))
            yk = jnp.concatenate([lo, hi], axis=1) * g4[:, k:k + 1]
            y = yk if y is None else y + yk
    o = 0
    if mode != "none":
        x = x + mod[i_gate:i_gate + 1, :] * _rms(y, gpost_ref[...])
        outs[o][...] = x
        o += 1
    if nxt != "none":
        hf = _rms(x, gpre_ref[...]) * (1.0 + modn[i_scale:i_scale + 1, :]) + modn[i_shift:i_shift + 1, :]
        if nxt == "plain":
            outs[o][...] = hf.astype(BF16)
        else:
            _store_token_rows(outs[o], _pack_halves(hf))
            idx, gates = _router_topk(hf, wr_hi_ref, wr_lo_ref, br_ref)
            outs[o + 1][...] = idx
            outs[o + 2][...] = gates


def _post(x, cond, tab, tab_next, *, S, mode, nxt, y=None, y4=None, g4=None, g_post=None, g_pre=None,
          router=None, i_gate=0, i_shift=0, i_scale=0, tm=256):
    N, D = x.shape
    nb = S // tm
    rpt = D // 2 // LANES
    row = lambda i: (i, 0)
    args, specs = [x], [pl.BlockSpec((tm, D), row)]
    if mode == "dense":
        args.append(y)
        specs.append(pl.BlockSpec((tm, D), row))
    elif mode == "moe":
        args += [y4, g4]
        specs += [pl.BlockSpec((TOP_K, tm * rpt, LANES), lambda i: (0, i, 0)), pl.BlockSpec((tm, LANES), row)]
    args += [cond, tab, tab_next]
    specs += [pl.BlockSpec((1, N_MOD, D), lambda i: (i // nb, 0, 0)), pl.BlockSpec((N_MOD, D), lambda i: (0, 0)),
              pl.BlockSpec((N_MOD, D), lambda i: (0, 0))]
    if mode != "none":
        args.append(g_post.reshape(1, D))
        specs.append(pl.BlockSpec((1, D), lambda i: (0, 0)))
    if nxt != "none":
        args.append(g_pre.reshape(1, D))
        specs.append(pl.BlockSpec((1, D), lambda i: (0, 0)))
    if nxt == "router":
        args += list(router)
        specs += [pl.BlockSpec((D, LANES), lambda i: (0, 0)), pl.BlockSpec((D, LANES), lambda i: (0, 0)),
                  pl.BlockSpec((1, LANES), lambda i: (0, 0))]
    out_shape, out_specs = [], []
    if mode != "none":
        out_shape.append(jax.ShapeDtypeStruct((N, D), F32))
        out_specs.append(pl.BlockSpec((tm, D), row))
    if nxt == "plain":
        out_shape.append(jax.ShapeDtypeStruct((N, D), BF16))
        out_specs.append(pl.BlockSpec((tm, D), row))
    elif nxt == "router":
        out_shape += [jax.ShapeDtypeStruct((N * rpt, LANES), U32), jax.ShapeDtypeStruct((N, LANES), I32),
                      jax.ShapeDtypeStruct((N, LANES), F32)]
        out_specs += [pl.BlockSpec((tm * rpt, LANES), row), pl.BlockSpec((tm, LANES), row), pl.BlockSpec((tm, LANES), row)]
    return pl.pallas_call(
        functools.partial(_post_kernel, mode=mode, nxt=nxt, i_gate=i_gate, i_shift=i_shift, i_scale=i_scale),
        out_shape=out_shape, grid=(N // tm,), in_specs=specs, out_specs=out_specs,
        compiler_params=_cparams(("arbitrary",), 56),
        name=f"post_{mode}_{nxt}",
    )(*args)


def _mm_kernel(*refs, n_a):
    a_refs, w_ref, o_ref = refs[:n_a], refs[n_a], refs[n_a + 1]
    acc, off = None, 0
    for a in a_refs:
        kg = a.shape[1]
        part = jnp.dot(a[...], w_ref[0, off:off + kg, :].astype(BF16), preferred_element_type=F32)
        acc = part if acc is None else acc + part
        off += kg
    o_ref[...] = acc.astype(o_ref.dtype)


def _matmul(a_list, w, l, out_dtype, tm, tn):
    N = a_list[0].shape[0]
    K, n_out = w.shape[1:]
    specs = [pl.BlockSpec((tm, a.shape[1]), lambda i, j: (i, 0)) for a in a_list]
    specs.append(pl.BlockSpec((1, K, tn), lambda i, j: (l, 0, j)))
    return pl.pallas_call(
        functools.partial(_mm_kernel, n_a=len(a_list)),
        out_shape=jax.ShapeDtypeStruct((N, n_out), out_dtype),
        grid=(N // tm, n_out // tn), in_specs=specs,
        out_specs=pl.BlockSpec((tm, tn), lambda i, j: (i, j)),
        compiler_params=_cparams(("arbitrary", "arbitrary"), 56),
        name="matmul",
    )(*a_list, w)


def _cast_kernel(x_ref, o_ref):
    o_ref[...] = x_ref[0].astype(o_ref.dtype)


def _layer_bf16(w, l):
    C = w.shape[-1]
    R = math.prod(w.shape[1:-1])
    tr = math.gcd(R, max(8, pl.next_power_of_2(CAST_BLOCK_BYTES // (4 * C) + 1) // 2))
    out = pl.pallas_call(
        _cast_kernel,
        out_shape=jax.ShapeDtypeStruct((R, C), BF16),
        grid=(R // tr,),
        in_specs=[pl.BlockSpec((1, tr, C), lambda i: (l, i, 0))],
        out_specs=pl.BlockSpec((tr, C), lambda i: (i, 0)),
        compiler_params=_cparams(("arbitrary",), 40),
        name="cast_bf16",
    )(w.reshape(w.shape[0], R, C))
    return out.reshape(w.shape[1:])


def _conv_kernel(val_ref, gate_ref, pval_ref, pgate_ref, w_ref, b_ref, lg_ref, lb_ref, o_ref, ubuf, ybuf, shbuf, *, ts):
    G = o_ref.shape[1]
    i = pl.program_id(1)
    v = val_ref[...].astype(F32)
    g = gate_ref[...].astype(F32)
    ubuf[CONV_HALO:CONV_HALO + ts, :] = v * jax.nn.sigmoid(g)
    pv = pval_ref[...].astype(F32)
    pg = pgate_ref[...].astype(F32)
    ubuf[0:CONV_HALO, :] = jnp.where(i > 0, pv * jax.nn.sigmoid(pg), 0.0)
    base = CONV_HALO - (CONV_WIDTH - 1)
    rs = 128
    span = ts + CONV_HALO - SUBLANES

    def chan(c, carry):
        c0 = pl.multiple_of(c * LANES, LANES)
        win = ubuf[:, pl.ds(c0, LANES)]
        for s in range(1, SUBLANES):
            shbuf[s - 1, 0:span, :] = win[s:s + span]
        for r0 in range(0, ts, rs):
            acc = jnp.zeros((rs, LANES), F32)
            for j in range(CONV_WIDTH):
                q, s = divmod(base + j, SUBLANES)
                a0 = r0 + SUBLANES * q
                src = ubuf[a0:a0 + rs, pl.ds(c0, LANES)] if s == 0 else shbuf[s - 1, a0:a0 + rs, :]
                acc = acc + w_ref[j:j + 1, pl.ds(c0, LANES)] * src
            ybuf[r0:r0 + rs, pl.ds(c0, LANES)] = acc
        return carry

    lax.fori_loop(0, G // LANES, chan, 0)
    for r0 in range(0, ts, 64):
        y = ybuf[r0:r0 + 64, :] + b_ref[...]
        mu = jnp.mean(y, axis=-1, keepdims=True)
        d = y - mu
        var = jnp.mean(d * d, axis=-1, keepdims=True)
        z = d * lax.rsqrt(var + LN_EPS) * lg_ref[...] + lb_ref[...]
        o_ref[r0:r0 + 64, :] = (z * jax.nn.sigmoid(z)).astype(o_ref.dtype)


def _conv_mixer(p, conv_w, conv_b, ln_g, ln_b, *, B, S, G, ts=256):
    nb = S // ts
    hb = ts // CONV_HALO
    wpad = jnp.zeros((CONV_HALO, G), F32).at[:CONV_WIDTH].set(conv_w)
    cur = lambda col: pl.BlockSpec((ts, G), lambda b, i: (b * nb + i, col))
    prev = lambda col: pl.BlockSpec((CONV_HALO, G), lambda b, i: (jnp.maximum((b * nb + i) * hb - 1, 0), col))
    vec = pl.BlockSpec((1, G), lambda b, i: (0, 0))
    return pl.pallas_call(
        functools.partial(_conv_kernel, ts=ts),
        out_shape=jax.ShapeDtypeStruct((B * S, G), BF16),
        grid=(B, nb),
        in_specs=[cur(0), cur(1), prev(0), prev(1), pl.BlockSpec((CONV_HALO, G), lambda b, i: (0, 0)), vec, vec, vec],
        out_specs=pl.BlockSpec((ts, G), lambda b, i: (b * nb + i, 0)),
        scratch_shapes=[pltpu.VMEM((ts + CONV_HALO, G), F32), pltpu.VMEM((ts, G), F32),
                        pltpu.VMEM((SUBLANES - 1, ts + CONV_HALO, LANES), F32)],
        compiler_params=_cparams(("arbitrary", "arbitrary"), 32),
        name="conv_mixer",
    )(p, p, p, p, wpad, conv_b.reshape(1, G), ln_g.reshape(1, G), ln_b.reshape(1, G))


_NT = (((1,), (1,)), ((), ()))
SB_HEADS, DIFF_HEADS, MOBA_HEADS = 4, 2, 4
SB_UNDERFLOW = 110.0
DIFF_TILE = 512
MOBA_GROUP = 4
CAST_BLOCK_BYTES = 8 * MIB
ROWCOPY_CHUNK = 2048
ISSUE_UNROLL = 8
WAIT_GROUP = 64
LOG2E = 1.4426950408889634


def _hs(hh):
    return slice(hh * HEAD_DIM, (hh + 1) * HEAD_DIM)


def _sb_kernel(q_ref, k_ref, v_ref, o_ref, *, t, nh):
    i = pl.program_id(2)
    qs = [q_ref[:, _hs(hh)] for hh in range(nh)]
    scale = HEAD_DIM ** -0.5
    row = lax.broadcasted_iota(I32, (t, t), 0)
    col = lax.broadcasted_iota(I32, (t, t), 1)
    tri = jnp.where(row >= col, 1.0, 0.0).astype(BF16)
    past = col < row

    def block(j, st, diag):
        j0 = pl.multiple_of(j * t, t)
        zs = [lax.dot_general(qs[hh], k_ref[pl.ds(j0, t), _hs(hh)], _NT, preferred_element_type=F32) * scale
              for hh in range(nh)]
        his, los = [], []
        for hh in range(nh):
            z = zs[hh]
            sp = jnp.maximum(z, 0.0) + jnp.log(1.0 + jnp.exp(-jnp.abs(z)))
            u = jnp.where(past, sp, 0.0) if diag else sp
            u_hi = u.astype(BF16)
            his.append(u_hi)
            los.append((u - u_hi.astype(F32)).astype(BF16))
        tls = [jnp.dot(his[hh], tri, preferred_element_type=F32) + jnp.dot(los[hh], tri, preferred_element_type=F32)
               for hh in range(nh)]
        a_s = []
        for hh in range(nh):
            e = zs[hh] - (tls[hh] + st[hh][0])
            if diag:
                e = jnp.where(past, e, -jnp.inf)
            a_s.append(jnp.exp(e).astype(BF16))
        return tuple((st[hh][0] + tls[hh][:, 0:1],
                      st[hh][1] + jnp.dot(a_s[hh], v_ref[pl.ds(j0, t), _hs(hh)], preferred_element_type=F32))
                     for hh in range(nh))

    def live(st):
        lo = st[0][0]
        for hh in range(1, nh):
            lo = jnp.minimum(lo, st[hh][0])
        return (jnp.min(lo) < SB_UNDERFLOW).astype(I32)

    def body(c):
        st = block(i - 1 - c[0], c[2], False)
        return c[0] + 1, live(st), st

    init = tuple((jnp.zeros((t, 1), F32), jnp.zeros((t, HEAD_DIM), F32)) for _ in range(nh))
    st = block(i, init, True)
    _, _, st = lax.while_loop(lambda c: (c[0] < i) & (c[1] > 0), body, (jnp.int32(0), live(st), st))
    for hh in range(nh):
        o_ref[:, _hs(hh)] = st[hh][1].astype(o_ref.dtype)


def _diff_kernel(q_ref, k_ref, v_ref, dl_ref, g_ref, o_ref, *, t, nh, lam_init):
    i = pl.program_id(2)
    scale = DIFF_QK_DIM ** -0.5
    assert math.frexp(scale)[0] == 0.5
    lane = lax.broadcasted_iota(I32, (t, HEAD_DIM), 1)
    qs = []
    for hh in range(nh):
        q = q_ref[:, _hs(hh)] * scale
        zero = jnp.zeros_like(q)
        qs.append((jnp.where(lane < DIFF_QK_DIM, q, zero), jnp.where(lane >= DIFF_QK_DIM, q, zero)))
    row = lax.broadcasted_iota(I32, (t, t), 0)
    col = lax.broadcasted_iota(I32, (t, t), 1)
    causal = col <= row
    dl = dl_ref[...]
    lam = (jnp.exp(jnp.sum(dl[0:1] * dl[1:2], axis=1, keepdims=True))
           - jnp.exp(jnp.sum(dl[2:3] * dl[3:4], axis=1, keepdims=True)) + lam_init)

    def block(j, st, diag):
        j0 = pl.multiple_of(j * t, t)
        ss = [lax.dot_general(qs[hh][mi], k_ref[pl.ds(j0, t), _hs(hh)], _NT, preferred_element_type=F32)
              for hh in range(nh) for mi in range(2)]
        prs, alphas, ls, ms = [], [], [], []
        for ci in range(2 * nh):
            m, l, acc = st[ci]
            s = jnp.where(causal, ss[ci], NEG_BIG) if diag else ss[ci]
            m_new = jnp.maximum(m, jnp.max(s, axis=1, keepdims=True))
            alpha = jnp.exp2((m - m_new) * LOG2E)
            pr = jnp.exp2((s - m_new) * LOG2E)
            ls.append(alpha * l + jnp.sum(pr, axis=1, keepdims=True))
            prs.append(pr.astype(BF16))
            alphas.append(alpha)
            ms.append(m_new)
        new = []
        for ci in range(2 * nh):
            vj = v_ref[pl.ds(j0, t), _hs(ci // 2)]
            acc = alphas[ci] * st[ci][2] + jnp.dot(prs[ci], vj, preferred_element_type=F32)
            new.append((ms[ci], ls[ci], acc))
        return tuple(new)

    init = tuple((jnp.full((t, 1), NEG_BIG, F32), jnp.zeros((t, 1), F32), jnp.zeros((t, HEAD_DIM), F32))
                 for _ in range(2 * nh))
    st = block(i, init, True)
    st = lax.fori_loop(0, i, lambda j, s: block(j, s, False), st)
    for hh in range(nh):
        (_, l0, a0), (_, l1, a1) = st[2 * hh], st[2 * hh + 1]
        o = a0 / l0 - lam * (a1 / l1)
        o_ref[:, _hs(hh)] = (_rms(o, g_ref[...]) * (1.0 - lam_init)).astype(o_ref.dtype)


def _moba_kernel(q_ref, k_ref, v_ref, o_ref, km_hi, km_lo, *, t, nh, nkb, grp):
    own = pl.program_id(2)
    scale = HEAD_DIM ** -0.5
    topk = min(MOBA_TOPK, nkb)

    @pl.when(own == 0)
    def _():
        km_hi[...] = jnp.zeros_like(km_hi)
        km_lo[...] = jnp.zeros_like(km_lo)
        for hh in range(nh):
            for blk in range(nkb):
                km = jnp.mean(k_ref[blk * t:(blk + 1) * t, _hs(hh)].astype(F32), axis=0, keepdims=True)
                hi = km.astype(BF16)
                km_hi[hh, blk:blk + 1, :] = hi
                km_lo[hh, blk:blk + 1, :] = (km - hi.astype(F32)).astype(BF16)

    lane = lax.broadcasted_iota(I32, (t, LANES), 1)
    lane_f = lane.astype(F32)
    row = lax.broadcasted_iota(I32, (t, t), 0)
    col = lax.broadcasted_iota(I32, (t, t), 1)
    qs, sels = [], []
    for hh in range(nh):
        q = q_ref[:, _hs(hh)]
        gate = (lax.dot_general(q, km_hi[hh], _NT, preferred_element_type=F32)
                + lax.dot_general(q, km_lo[hh], _NT, preferred_element_type=F32))
        cand = jnp.where(lane < own, gate, -jnp.inf)
        sel = jnp.zeros(gate.shape, F32)
        for _ in range(topk):
            best = jnp.max(cand, axis=1, keepdims=True)
            first = jnp.min(jnp.where(cand == best, lane_f, float(LANES)), axis=1, keepdims=True)
            hit = (lane_f == first) & (best > -jnp.inf)
            sel = jnp.where(hit, 1.0, sel)
            cand = jnp.where(hit, -jnp.inf, cand)
        qs.append(q)
        sels.append(sel)

    def attend(j, nblk, st, masks):
        j0 = pl.multiple_of(j * t, t)
        c = scale * LOG2E
        ss = [lax.dot_general(qs[hh], k_ref[pl.ds(j0, nblk * t), _hs(hh)], _NT, preferred_element_type=F32)
              for hh in range(nh)]
        prs, alphas, ls, ms = [], [], [], []
        for hh in range(nh):
            m, l, acc = st[hh]
            s = ss[hh]
            s = jnp.concatenate([jnp.where(masks[hh][b], s[:, b * t:(b + 1) * t], NEG_BIG) for b in range(nblk)], axis=1)
            m_new = jnp.maximum(m, jnp.max(s, axis=1, keepdims=True))
            alpha = jnp.exp2((m - m_new) * c)
            pr = jnp.exp2((s - m_new) * c)
            ls.append(alpha * l + jnp.sum(pr, axis=1, keepdims=True))
            prs.append(pr.astype(BF16))
            alphas.append(alpha)
            ms.append(m_new)
        new = []
        for hh in range(nh):
            acc = alphas[hh] * st[hh][2] + jnp.dot(prs[hh], v_ref[pl.ds(j0, nblk * t), _hs(hh)], preferred_element_type=F32)
            new.append((ms[hh], ls[hh], acc))
        return tuple(new)

    init = tuple((jnp.full((t, 1), NEG_BIG, F32), jnp.zeros((t, 1), F32), jnp.zeros((t, HEAD_DIM), F32))
                 for _ in range(nh))
    st = attend(own, 1, init, [[col <= row]] * nh)

    def body(g, st):
        picked = [[jnp.max(jnp.where(lane == g * grp + b, sels[hh], 0.0), axis=1, keepdims=True) > 0.5
                   for b in range(grp)] for hh in range(nh)]
        return attend(g * grp, grp, st, picked)

    st = lax.fori_loop(0, (own + grp - 1) // grp, body, st)
    for hh in range(nh):
        _, l, acc = st[hh]
        o_ref[:, _hs(hh)] = (acc / l).astype(o_ref.dtype)


def _attention(kernel_fn, p, extra, extra_specs, scratch, *, B, S, G, c_q, c_k, c_v, t, nh, name):
    H = G // HEAD_DIM
    nq = S // t
    w = nh * HEAD_DIM
    assert H % nh == 0 and c_q % nh == 0 and c_k % nh == 0 and c_v % nh == 0
    return pl.pallas_call(
        kernel_fn,
        out_shape=jax.ShapeDtypeStruct((B * S, G), BF16),
        grid=(B, H // nh, nq),
        in_specs=[pl.BlockSpec((t, w), lambda b, h, i: (b * nq + i, c_q // nh + h)),
                  pl.BlockSpec((S, w), lambda b, h, i: (b, c_k // nh + h)),
                  pl.BlockSpec((S, w), lambda b, h, i: (b, c_v // nh + h))] + extra_specs,
        out_specs=pl.BlockSpec((t, w), lambda b, h, i: (b * nq + i, h)),
        scratch_shapes=scratch,
        compiler_params=_cparams(("arbitrary", "arbitrary", "arbitrary"), 40),
        name=name,
    )(p, p, p, *extra)


def _moe_plan(top_idx, N, tm):
    E, NK = N_EXPERTS, N * TOP_K
    nblocks = (NK + E * (tm - 1) + tm - 1) // tm
    e_flat = top_idx.reshape(NK)
    order = jnp.argsort(e_flat, stable=True).astype(I32)
    counts = jnp.sum((e_flat[:, None] == jnp.arange(E, dtype=I32)[None, :]).astype(I32), axis=0)
    starts = jnp.cumsum(counts) - counts
    nblk_e = (counts + tm - 1) // tm
    bend = jnp.cumsum(nblk_e)
    bstart = bend - nblk_e
    b = jnp.arange(nblocks, dtype=I32)
    blk_e = jnp.minimum(jnp.sum((bend[None, :] <= b[:, None]).astype(I32), axis=1), E - 1).astype(I32)
    blk_off = (b - bstart[blk_e]) * tm
    blk_nv = jnp.where(b < bend[-1], jnp.clip(counts[blk_e] - blk_off, 0, tm), 0).astype(I32)
    ri = jnp.arange(tm, dtype=I32)
    valid = ri[None, :] < blk_nv[:, None]
    src = jnp.clip(starts[blk_e][:, None] + blk_off[:, None] + ri[None, :], 0, NK - 1)
    flat = order[src]
    row_tok = jnp.where(valid, flat // TOP_K, 0).astype(I32)
    row_slot = jnp.where(valid, (flat % TOP_K) * N + flat // TOP_K, -1).astype(I32)
    return blk_e, blk_nv, row_tok, row_slot


def _rowcopy_kernel(cnt_ref, idx_hbm, src, dst, idx_smem, isem, csem, *, ch, rpt, gather):
    s = pl.program_id(0)
    ns = pl.num_programs(0)
    slot = s % 2

    def idx_copy(step, sl):
        return pltpu.make_async_copy(idx_hbm.at[step], idx_smem.at[sl], isem.at[sl])

    @pl.when(s == 0)
    def _():
        idx_copy(0, 0).start()

    idx_copy(s, slot).wait()

    @pl.when(s + 1 < ns)
    def _():
        idx_copy(s + 1, 1 - slot).start()

    def row_copy(r, t):
        blk = pl.ds(pl.multiple_of(r * rpt, rpt), rpt)
        far = pl.ds(pl.multiple_of(t * rpt, rpt), rpt)
        if gather:
            return pltpu.make_async_copy(src.at[far], dst.at[blk], csem)
        return pltpu.make_async_copy(src.at[blk], dst.at[far], csem)

    def issue(g, c):
        for u in range(ISSUE_UNROLL):
            r = g * ISSUE_UNROLL + u
            t = idx_smem[slot, r]
            if gather:
                row_copy(r, t).start(priority=u % 2)
            else:
                @pl.when(t >= 0)
                def _():
                    row_copy(r, t).start(priority=u % 2)
        return c

    def drain_group(g, c):
        for _ in range(WAIT_GROUP):
            row_copy(0, 0).wait()
        return c

    def drain_one(r, c):
        row_copy(0, 0).wait()
        return c

    lax.fori_loop(0, ch // ISSUE_UNROLL, issue, 0)
    n = cnt_ref[s]
    lax.fori_loop(0, n // WAIT_GROUP, drain_group, 0)
    lax.fori_loop(0, n % WAIT_GROUP, drain_one, 0)


def _rowcopy(idx, src, n_dst_tokens, *, rpt, gather, ch=ROWCOPY_CHUNK):
    ch = math.gcd(ch, idx.shape[0])
    ns = idx.shape[0] // ch
    idx2 = idx.reshape(ns, ch)
    cnt = jnp.sum((idx2 >= 0).astype(I32), axis=1)
    blk = pl.BlockSpec((ch * rpt, LANES), lambda s, cnt: (s, 0))
    hbm = pl.BlockSpec(memory_space=pl.ANY)
    return pl.pallas_call(
        functools.partial(_rowcopy_kernel, ch=ch, rpt=rpt, gather=gather),
        out_shape=jax.ShapeDtypeStruct((n_dst_tokens * rpt, LANES), src.dtype),
        grid_spec=pltpu.PrefetchScalarGridSpec(
            num_scalar_prefetch=1, grid=(ns,),
            in_specs=[hbm, hbm if gather else blk],
            out_specs=blk if gather else hbm,
            scratch_shapes=[pltpu.SMEM((2, ch), I32), pltpu.SemaphoreType.DMA((2,)), pltpu.SemaphoreType.DMA]),
        compiler_params=_cparams(("arbitrary",), 48),
        name="row_gather" if gather else "row_scatter",
    )(cnt, idx2, src)


def _expert_kernel(be_ref, nv_ref, tok_hbm, h_hbm, w1_ref, b1_ref, w2_ref, b2_ref, o_ref, tok_smem, tsem, xbuf, gsem,
                   *, tm, rpt):
    b = pl.program_id(0)
    nb = pl.num_programs(0)
    slot = b % 2
    F = w2_ref.shape[1]
    half = w1_ref.shape[1] // 2

    def tok_copy(blk, sl):
        return pltpu.make_async_copy(tok_hbm.at[blk], tok_smem.at[sl], tsem.at[sl])

    def row_copy(sl, r, t):
        return pltpu.make_async_copy(h_hbm.at[pl.ds(pl.multiple_of(t * rpt, rpt), rpt)],
                                     xbuf.at[sl, pl.ds(pl.multiple_of(r * rpt, rpt), rpt)], gsem.at[sl])

    def start_gather(sl, inline=False):
        if inline:
            for r in range(tm):
                row_copy(sl, r, tok_smem[sl, r]).start(priority=r % 2)
            return

        def issue(g, c):
            for u in range(ISSUE_UNROLL):
                r = g * ISSUE_UNROLL + u
                row_copy(sl, r, tok_smem[sl, r]).start(priority=u % 2)
            return c
        lax.fori_loop(0, tm // ISSUE_UNROLL, issue, 0)

    def wait_gather(sl):
        def drain(g, c):
            for _ in range(WAIT_GROUP):
                row_copy(sl, 0, 0).wait()
            return c
        lax.fori_loop(0, tm // WAIT_GROUP, drain, 0)

    @pl.when(b == 0)
    def _():
        tok_copy(0, 0).start()
        tok_copy(0, 0).wait()
        start_gather(0)
        tok_copy(1, 1).start()

    @pl.when(b + 1 < nb)
    def _():
        tok_copy(b + 1, 1 - slot).wait()

    wait_gather(slot)

    @pl.when(b + 2 < nb)
    def _():
        tok_copy(b + 2, slot).start()

    @pl.when(nv_ref[b] > 0)
    def _():
        start_gather(1 - slot, inline=True)
        lo, hi = _unpack_halves(_load_token_rows(xbuf.at[slot], tm))
        a = (jnp.dot(lo, w1_ref[0, :half, :], preferred_element_type=F32)
             + jnp.dot(hi, w1_ref[0, half:, :], preferred_element_type=F32) + b1_ref[0])
        g = jnp.minimum(a[:, :F], SWIGLU_LIMIT)
        lin = jnp.clip(a[:, F:], -SWIGLU_LIMIT, SWIGLU_LIMIT)
        act = g * jax.nn.sigmoid(SWIGLU_ALPHA * g) * (lin + 1.0)
        y = jnp.dot(act.astype(BF16), w2_ref[0], preferred_element_type=F32) + b2_ref[0]
        _store_token_rows(o_ref, _pack_halves(y))

    @pl.when(nv_ref[b] == 0)
    def _():
        start_gather(1 - slot)
        o_ref[...] = jnp.zeros_like(o_ref)

    @pl.when(b == nb - 1)
    def _():
        wait_gather(1 - slot)


def _experts(blk_e, blk_nv, row_tok, hpk, w1, b1, w2, b2, *, tm, rpt):
    E, D, F2 = w1.shape
    F = F2 // 2
    nblocks = row_tok.shape[0]
    assert nblocks >= 2
    hbm = pl.BlockSpec(memory_space=pl.ANY)
    return pl.pallas_call(
        functools.partial(_expert_kernel, tm=tm, rpt=rpt),
        out_shape=jax.ShapeDtypeStruct((nblocks * tm * rpt, LANES), U32),
        grid_spec=pltpu.PrefetchScalarGridSpec(
            num_scalar_prefetch=2, grid=(nblocks,),
            in_specs=[hbm, hbm,
                      pl.BlockSpec((1, D, F2), lambda b, be, nv: (be[b], 0, 0)),
                      pl.BlockSpec((1, 1, F2), lambda b, be, nv: (be[b], 0, 0)),
                      pl.BlockSpec((1, F, D), lambda b, be, nv: (be[b], 0, 0)),
                      pl.BlockSpec((1, 1, D), lambda b, be, nv: (be[b], 0, 0))],
            out_specs=pl.BlockSpec((tm * rpt, LANES), lambda b, be, nv: (b, 0)),
            scratch_shapes=[pltpu.SMEM((2, tm), I32), pltpu.SemaphoreType.DMA((2,)),
                            pltpu.VMEM((2, tm * rpt, LANES), U32), pltpu.SemaphoreType.DMA((2,))]),
        compiler_params=_cparams(("arbitrary",), 56),
        name="experts",
    )(blk_e, blk_nv, row_tok, hpk, w1, b1.reshape(E, 1, F2), w2, b2.reshape(E, 1, D))


def _moe(hpk, top_idx, w1, b1, w2, b2, *, N, rpt, tm=256):
    blk_e, blk_nv, row_tok, row_slot = _moe_plan(top_idx, N, tm)
    ys = _experts(blk_e, blk_nv, row_tok, hpk, w1, b1, w2, b2, tm=tm, rpt=rpt)
    return _rowcopy(row_slot.reshape(-1), ys, TOP_K * N, rpt=rpt, gather=False)


def kernel(x, c, w_ada, b_ada, ada_table, norm_pre, norm_post, w_in, w_out, conv_w, conv_b, conv_ln_g, conv_ln_b,
           diff_lambda, diff_norm_g, w_router, b_router, w1, b1, w2, b2):
    B, S, D = x.shape
    depth = w_in.shape[0]
    N = B * S
    G = D // N_GROUPS
    H = G // HEAD_DIM
    E = w_router.shape[2]
    assert E == N_EXPERTS and S % MOBA_BLOCK == 0 and G % HEAD_DIM == 0
    nkb = S // MOBA_BLOCK

    cond = _ada(c, w_ada, b_ada)
    xf = x.reshape(N, D)
    h = None
    for l in range(depth):
        lam_init = 0.8 - 0.6 * math.exp(-0.3 * l)
        tab = ada_table[l]
        if l == 0:
            (h,) = _post(xf, cond, tab, tab, S=S, mode="none", nxt="plain", g_pre=norm_pre[l, 0], i_shift=0, i_scale=1)
        p = _matmul([h], w_in, l, BF16, 1024, 512 if w_in.shape[2] % 512 == 0 else G)

        y_conv = _conv_mixer(p, conv_w[l], conv_b[l], conv_ln_g[l], conv_ln_b[l], B=B, S=S, G=G)
        t = MOBA_BLOCK
        nh_sb, nh_diff, nh_moba = (math.gcd(n, H) for n in (SB_HEADS, DIFF_HEADS, MOBA_HEADS))
        td = math.gcd(DIFF_TILE, S)
        y_sb = _attention(functools.partial(_sb_kernel, t=t, nh=nh_sb), p, [], [], [], B=B, S=S, G=G,
                          c_q=2 * H, c_k=3 * H, c_v=4 * H, t=t, nh=nh_sb, name="stick_breaking")
        y_diff = _attention(
            functools.partial(_diff_kernel, t=td, nh=nh_diff, lam_init=lam_init), p,
            [diff_lambda[l], diff_norm_g[l].reshape(1, HEAD_DIM)],
            [pl.BlockSpec((4, DIFF_QK_DIM), lambda b, h_, i: (0, 0)), pl.BlockSpec((1, HEAD_DIM), lambda b, h_, i: (0, 0))],
            [], B=B, S=S, G=G, c_q=5 * H, c_k=6 * H, c_v=7 * H, t=td, nh=nh_diff, name="diff_attention")
        y_moba = _attention(
            functools.partial(_moba_kernel, t=t, nh=nh_moba, nkb=nkb, grp=math.gcd(MOBA_GROUP, nkb)), p, [], [],
            [pltpu.VMEM((nh_moba, LANES, HEAD_DIM), BF16), pltpu.VMEM((nh_moba, LANES, HEAD_DIM), BF16)],
            B=B, S=S, G=G, c_q=8 * H, c_k=9 * H, c_v=10 * H, t=t, nh=nh_moba, name="moba")

        y = _matmul([y_conv, y_sb, y_diff, y_moba], w_out, l, BF16, 1024, 512)

        wr = jnp.zeros((D, LANES), F32).at[:, :E].set(w_router[l])
        wr_hi = wr.astype(BF16)
        wr_lo = (wr - wr_hi.astype(F32)).astype(BF16)
        br = jnp.zeros((1, LANES), F32).at[0, :E].set(b_router[l])
        xf, hpk, top_idx, gates = _post(xf, cond, tab, tab, S=S, mode="dense", nxt="router", y=y, g_post=norm_post[l, 0],
                                        g_pre=norm_pre[l, 1], router=(wr_hi, wr_lo, br), i_gate=2, i_shift=3, i_scale=4)

        rpt = D // 2 // LANES
        y4 = _moe(hpk, top_idx[:, :TOP_K], _layer_bf16(w1, l), b1[l], _layer_bf16(w2, l), b2[l], N=N, rpt=rpt)
        y4 = y4.reshape(TOP_K, N * rpt, LANES)
        if l + 1 < depth:
            xf, h = _post(xf, cond, tab, ada_table[l + 1], S=S, mode="moe", nxt="plain", y4=y4, g4=gates,
                          g_post=norm_post[l, 1], g_pre=norm_pre[l + 1, 0], i_gate=5, i_shift=0, i_scale=1)
        else:
            (xf,) = _post(xf, cond, tab, tab, S=S, mode="moe", nxt="none", y4=y4, g4=gates, g_post=norm_post[l, 1], i_gate=5)
    return xf.reshape(B, S, D)
```
